```python
import jax, jax.numpy as jnp
from jax import lax
import numpy as np

D_MODEL = 2048
BATCH = 2
SEQ = 16384
DEPTH = 2

N_MEM = 256
EPS = 1e-6
N_EVEN = (DEPTH + 1) // 2
N_ODD = DEPTH // 2

SB_HEADS = 4
SB_HEAD_DIM = 256
SB_WIDTH = SB_HEADS * SB_HEAD_DIM
SB_BLOCK = 128
SB_MASK_VALUE = -1e4
POOL_WINDOWS = (2, 4, 8, 16)
POOL_GROUPS = len(POOL_WINDOWS)
POOL_WIDTH = D_MODEL - SB_WIDTH
POOL_GROUP_DIM = POOL_WIDTH // POOL_GROUPS
EVEN_IN = 3 * SB_WIDTH + POOL_WIDTH

GLA_HEADS = 4
GLA_KEY_DIM = D_MODEL // 2
GLA_VAL_DIM = D_MODEL
GLA_HK = GLA_KEY_DIM // GLA_HEADS
GLA_HV = GLA_VAL_DIM // GLA_HEADS
GLA_GATE_RANK = 16
GLA_GATE_NORMALIZER = 16.0
GLA_CHUNK = 64
ODD_IN = 2 * GLA_KEY_DIM + 2 * GLA_VAL_DIM + GLA_GATE_RANK

X_HEADS = 4
X_HEAD_DIM = D_MODEL // X_HEADS

D_FF = -(-8 * D_MODEL // (3 * 256)) * 256

kernel_name = "stickbreak_pool_gla_hybrid_trunk"

F32 = jnp.float32


def rmsnorm(x, g):
    xf = x.astype(F32)
    y = xf * lax.rsqrt(jnp.mean(xf * xf, axis=-1, keepdims=True) + EPS)
    return (y * g.astype(F32)).astype(x.dtype)


def stick_breaking_attention(q, k, v):
    B, S, H, dh = q.shape
    C = SB_BLOCK
    nblk = S // C
    qh = q.transpose(0, 2, 1, 3).astype(F32) * (dh ** -0.5)
    kh = k.transpose(0, 2, 1, 3).astype(F32)
    vh = v.transpose(0, 2, 1, 3).astype(F32)
    ar = jnp.arange(C)
    later_in_block = (ar[:, None] > ar[None, :]).astype(F32)
    outs = []
    for i in range(nblk):
        nk = i + 1
        Lk = nk * C
        qb = qh[:, :, i * C:(i + 1) * C]
        kb = kh[:, :, :Lk]
        vb = vh[:, :, :Lk]
        z = jnp.einsum('bhqd,bhkd->bhqk', qb, kb)
        q_pos = i * C + ar
        mask = jnp.arange(Lk)[None, :] < q_pos[:, None]
        z = jnp.where(mask, z, SB_MASK_VALUE).reshape(B, H, C, nk, C)
        log_stay = jax.nn.log_sigmoid(-z)
        within = jnp.einsum('bhqcj,js->bhqcs', log_stay, later_in_block)
        blk_tot = jnp.sum(log_stay, axis=-1)
        bi = jnp.arange(nk)
        later_blocks = (bi[:, None] > bi[None, :]).astype(F32)
        after = jnp.einsum('bhqc,cd->bhqd', blk_tot, later_blocks)
        w = jnp.exp(z + log_stay + within + after[..., None]).reshape(B, H, C, Lk)
        outs.append(jnp.einsum('bhqk,bhkd->bhqd', w, vb))
    out = jnp.concatenate(outs, axis=2)
    return out.transpose(0, 2, 1, 3).astype(q.dtype)


def multiscale_pool(u, pool_w, pool_scale):
    B, S, C = u.shape
    groups = u.astype(F32).reshape(B, S, POOL_GROUPS, POOL_GROUP_DIM)
    cs = lax.associative_scan(jnp.add, groups, axis=1)
    pos = jnp.arange(S)
    pooled = []
    for gi, w in enumerate(POOL_WINDOWS):
        cg = cs[:, :, gi]
        shifted = jnp.pad(cg[:, :S - w], ((0, 0), (w, 0), (0, 0)))
        count = jnp.minimum(pos + 1, w).astype(F32)
        mean = (cg - shifted) / count[None, :, None]
        pooled.append(mean - groups[:, :, gi])
    p = jnp.stack(pooled, axis=2)
    y = jnp.einsum('bsgc,gcd->bsgd', p, pool_w.astype(F32)).reshape(B, S, C)
    return (y * pool_scale.astype(F32)).astype(u.dtype)


def gla_chunked(q, k, v, g):
    B, S, H, dk = q.shape
    dv = v.shape[-1]
    L = GLA_CHUNK
    n = S // L

    def chunks(t):
        return t.astype(F32).reshape(B, n, L, H, t.shape[-1]).transpose(0, 3, 1, 2, 4)

    q = chunks(q) * (dk ** -0.5)
    k = chunks(k)
    v = chunks(v)
    b = jnp.cumsum(chunks(g), axis=3)
    b_ref = b[:, :, :, L // 2:L // 2 + 1]
    b_last = b[:, :, :, L - 1:]

    qi = q * jnp.exp(b - b_ref)
    ki = k * jnp.exp(b_ref - b)
    scores = jnp.einsum('bhnld,bhnmd->bhnlm', qi, ki)
    causal = jnp.tril(jnp.ones((L, L), dtype=bool))
    scores = jnp.where(causal, scores, 0.0)
    o_intra = jnp.einsum('bhnlm,bhnme->bhnle', scores, v)

    q_inter = q * jnp.exp(b)
    k_state = k * jnp.exp(b_last - b)
    decay = jnp.exp(b_last[..., 0, :])

    def step(state, xs):
        qc, kc, vc, dc = xs
        o = jnp.einsum('bhld,bhde->bhle', qc, state)
        state = dc[..., None] * state + jnp.einsum('bhld,bhle->bhde', kc, vc)
        return state, o

    xs = (jnp.moveaxis(q_inter, 2, 0), jnp.moveaxis(k_state, 2, 0),
          jnp.moveaxis(v, 2, 0), jnp.moveaxis(decay, 2, 0))
    state0 = jnp.zeros((B, H, dk, dv), F32)
    _, o_inter = lax.scan(step, state0, xs)
    o = o_intra + jnp.moveaxis(o_inter, 0, 2)
    return o.transpose(0, 2, 3, 1, 4).reshape(B, S, H, dv)


def even_mixer(h, w_in, pool_w, pool_scale, w_out):
    B, S, _ = h.shape
    proj = h @ w_in
    q, k, v, u = jnp.split(proj, [SB_WIDTH, 2 * SB_WIDTH, 3 * SB_WIDTH], axis=-1)
    shp = (B, S, SB_HEADS, SB_HEAD_DIM)
    o_sb = stick_breaking_attention(q.reshape(shp), k.reshape(shp), v.reshape(shp))
    o_pool = multiscale_pool(u, pool_w, pool_scale)
    return jnp.concatenate([o_sb.reshape(B, S, SB_WIDTH), o_pool], axis=-1) @ w_out


def odd_mixer(h, w_in, gate_w, gate_b, gnorm, w_out):
    B, S, _ = h.shape
    proj = h @ w_in
    K, V = GLA_KEY_DIM, GLA_VAL_DIM
    q, k, v, r, a = jnp.split(proj, [K, 2 * K, 2 * K + V, 2 * K + 2 * V], axis=-1)
    g = jax.nn.log_sigmoid((a @ gate_w + gate_b).astype(F32)) / GLA_GATE_NORMALIZER
    o = gla_chunked(q.reshape(B, S, GLA_HEADS, GLA_HK), k.reshape(B, S, GLA_HEADS, GLA_HK),
                    v.reshape(B, S, GLA_HEADS, GLA_HV), g.reshape(B, S, GLA_HEADS, GLA_HK))
    o = rmsnorm(o, gnorm) * jax.nn.silu(r.reshape(B, S, GLA_HEADS, GLA_HV).astype(F32))
    return o.reshape(B, S, V).astype(h.dtype) @ w_out


def memory_cross_attention(h, mem_n, wq, wk, wv, wo):
    B, S, _ = h.shape
    M = mem_n.shape[1]
    q = (h @ wq).reshape(B, S, X_HEADS, X_HEAD_DIM)
    k = (mem_n @ wk).reshape(B, M, X_HEADS, X_HEAD_DIM)
    v = (mem_n @ wv).reshape(B, M, X_HEADS, X_HEAD_DIM)
    s = jnp.einsum('bshd,bmhd->bhsm', q, k).astype(F32) * (X_HEAD_DIM ** -0.5)
    p = jax.nn.softmax(s, axis=-1).astype(v.dtype)
    o = jnp.einsum('bhsm,bmhd->bshd', p, v).reshape(B, S, X_HEADS * X_HEAD_DIM)
    return o @ wo


def swiglu(h, w_gate, w_up, w_down):
    return (jax.nn.silu(h @ w_gate) * (h @ w_up)) @ w_down


def setup_inputs(seed: int = 0) -> dict:
    key = jax.random.key(seed)
    ks = jax.random.split(key, 20)

    def nrm(k, shape, fan_in):
        return jax.random.normal(k, shape, F32) * (fan_in ** -0.5)

    D = D_MODEL
    return {
        "x": jax.random.normal(ks[0], (BATCH, SEQ, D), F32),
        "mem": jax.random.normal(ks[1], (BATCH, N_MEM, D), F32),
        "norms": 1.0 + 0.1 * jax.random.normal(ks[2], (DEPTH, 6, D), F32),
        "mem_norm": 1.0 + 0.1 * jax.random.normal(ks[3], (D,), F32),
        "even_w_in": nrm(ks[4], (N_EVEN, D, EVEN_IN), D),
        "pool_w": nrm(ks[5], (N_EVEN, POOL_GROUPS, POOL_GROUP_DIM, POOL_GROUP_DIM), POOL_GROUP_DIM),
        "pool_scale": 1.0 + 0.1 * jax.random.normal(ks[6], (N_EVEN, POOL_WIDTH), F32),
        "even_w_out": nrm(ks[7], (N_EVEN, D, D), D),
        "odd_w_in": nrm(ks[8], (N_ODD, D, ODD_IN), D),
        "gla_gate_w": nrm(ks[9], (N_ODD, GLA_GATE_RANK, GLA_KEY_DIM), GLA_GATE_RANK),
        "gla_gate_b": 0.1 * jax.random.normal(ks[10], (N_ODD, GLA_KEY_DIM), F32),
        "gla_gnorm": 1.0 + 0.1 * jax.random.normal(ks[11], (N_ODD, GLA_HV), F32),
        "odd_w_out": nrm(ks[12], (N_ODD, GLA_VAL_DIM, D), GLA_VAL_DIM),
        "xattn_wq": nrm(ks[13], (DEPTH, D, D), D),
        "xattn_wk": nrm(ks[14], (DEPTH, D, D), D),
        "xattn_wv": nrm(ks[15], (DEPTH, D, D), D),
        "xattn_wo": nrm(ks[16], (DEPTH, D, D), D),
        "ffn_w_gate": nrm(ks[17], (DEPTH, D, D_FF), D),
        "ffn_w_up": nrm(ks[18], (DEPTH, D, D_FF), D),
        "ffn_w_down": nrm(ks[19], (DEPTH, D_FF, D), D_FF),
    }


def reference(x, mem, norms, mem_norm, even_w_in, pool_w, pool_scale, even_w_out,
              odd_w_in, gla_gate_w, gla_gate_b, gla_gnorm, odd_w_out,
              xattn_wq, xattn_wk, xattn_wv, xattn_wo,
              ffn_w_gate, ffn_w_up, ffn_w_down):
    mem_n = rmsnorm(mem, mem_norm)
    for layer in range(DEPTH):
        g = norms[layer]
        i = layer // 2
        h = rmsnorm(x, g[0])
        if layer % 2 == 0:
            mix = even_mixer(h, even_w_in[i], pool_w[i], pool_scale[i], even_w_out[i])
        else:
            mix = odd_mixer(h, odd_w_in[i], gla_gate_w[i], gla_gate_b[i], gla_gnorm[i], odd_w_out[i])
        x = x + rmsnorm(mix, g[1])
        h = rmsnorm(x, g[2])
        x = x + rmsnorm(memory_cross_attention(h, mem_n, xattn_wq[layer], xattn_wk[layer],
                                               xattn_wv[layer], xattn_wo[layer]), g[3])
        h = rmsnorm(x, g[4])
        x = x + rmsnorm(swiglu(h, ffn_w_gate[layer], ffn_w_up[layer], ffn_w_down[layer]), g[5])
    return x
```

```python
import functools

import jax
import jax.numpy as jnp
from jax import lax
from jax.experimental import pallas as pl
from jax.experimental.pallas import tpu as pltpu

F32 = jnp.float32
BF16 = jnp.bfloat16

EPS = 1e-6

SB_HEADS = 4
SB_HEAD_DIM = 256
SB_WIDTH = SB_HEADS * SB_HEAD_DIM
SB_MASK_VALUE = -1e4
SB_LOG_UNDERFLOW = -88.0
POOL_WINDOWS = (2, 4, 8, 16)
POOL_HALO = 16
POOL_GROUP_DIM = 256

GLA_HEADS = 4
GLA_HK = 256
GLA_HV = 512
GLA_GATE_RANK = 16
GLA_GATE_NORMALIZER = 16.0
GLA_CHUNK = 64

X_HEADS = 4

V7X_VMEM_LIMIT_BYTES = 56 * 1024 * 1024
LANES = 128


def _params(*sem):
    return pltpu.CompilerParams(dimension_semantics=sem, vmem_limit_bytes=V7X_VMEM_LIMIT_BYTES)


def _rms(x, g):
    ms = jnp.mean(x * x, axis=-1, keepdims=True)
    return x * lax.rsqrt(ms + EPS) * g


def _dot(a, b):
    return jnp.dot(a, b, preferred_element_type=F32)


def _dot_nt(a, b):
    return lax.dot_general(a, b, (((1,), (1,)), ((), ())), preferred_element_type=F32)


def _dot_tn(a, b):
    return lax.dot_general(a, b, (((0,), (0,)), ((), ())), preferred_element_type=F32)


def _split_bf16(x):
    hi = x.astype(BF16)
    lo = (x - hi.astype(F32)).astype(BF16)
    return hi, lo


def _neg_softplus(x):
    return -(jnp.maximum(x, 0.0) + jnp.log1p(jnp.exp(-jnp.abs(x))))


def _norm_matmul_kernel(x_ref, g_ref, w_ref, *rest, has_extra):
    if has_extra:
        w2_ref, o_ref, o2_ref, h_ref = rest
    else:
        o_ref, h_ref = rest

    @pl.when(pl.program_id(1) == 0)
    def _():
        h_ref[...] = _rms(x_ref[...], g_ref[...]).astype(BF16)
        if has_extra:
            o2_ref[...] = _dot(h_ref[...], w2_ref[...]).astype(o2_ref.dtype)

    o_ref[...] = _dot(h_ref[...], w_ref[...]).astype(o_ref.dtype)


def norm_matmul(x, g, w, w_extra=None, *, tm, tn):
    m, k = x.shape
    n = w.shape[1]
    has_extra = w_extra is not None
    in_specs = [
        pl.BlockSpec((tm, k), lambda i, j: (i, 0)),
        pl.BlockSpec((1, k), lambda i, j: (0, 0)),
        pl.BlockSpec((k, tn), lambda i, j: (0, j)),
    ]
    out_specs = [pl.BlockSpec((tm, tn), lambda i, j: (i, j))]
    out_shape = [jax.ShapeDtypeStruct((m, n), BF16)]
    args = [x, g.reshape(1, k), w]
    if has_extra:
        n2 = w_extra.shape[1]
        in_specs.append(pl.BlockSpec((k, n2), lambda i, j: (0, 0)))
        out_specs.append(pl.BlockSpec((tm, n2), lambda i, j: (i, 0)))
        out_shape.append(jax.ShapeDtypeStruct((m, n2), BF16))
        args.append(w_extra)
    out = pl.pallas_call(
        functools.partial(_norm_matmul_kernel, has_extra=has_extra),
        grid=(m // tm, n // tn),
        in_specs=in_specs,
        out_specs=out_specs,
        out_shape=out_shape,
        scratch_shapes=[pltpu.VMEM((tm, k), BF16)],
        compiler_params=_params("parallel", "arbitrary"),
        name="norm_matmul",
    )(*args)
    return out if has_extra else out[0]


def _matmul_norm_res_kernel(*refs, n_in):
    a_refs = refs[:n_in]
    w_refs = refs[n_in:2 * n_in]
    g_ref, x_ref, o_ref = refs[2 * n_in:]
    y = _dot(a_refs[0][...], w_refs[0][...])
    for a_ref, w_ref in zip(a_refs[1:], w_refs[1:]):
        y = y + _dot(a_ref[...], w_ref[...])
    o_ref[...] = x_ref[...] + _rms(y, g_ref[...])


def matmul_norm_res(a_list, w_list, g, x, *, tm):
    m, n = x.shape
    n_in = len(a_list)
    in_specs = [pl.BlockSpec((tm, a.shape[1]), lambda i: (i, 0)) for a in a_list]
    in_specs += [pl.BlockSpec(w.shape, lambda i: (0, 0)) for w in w_list]
    in_specs += [pl.BlockSpec((1, n), lambda i: (0, 0)), pl.BlockSpec((tm, n), lambda i: (i, 0))]
    return pl.pallas_call(
        functools.partial(_matmul_norm_res_kernel, n_in=n_in),
        grid=(m // tm,),
        in_specs=in_specs,
        out_specs=pl.BlockSpec((tm, n), lambda i: (i, 0)),
        out_shape=jax.ShapeDtypeStruct((m, n), F32),
        compiler_params=_params("parallel"),
        name="matmul_norm_res",
    )(*a_list, *w_list, g.reshape(1, n), x)


def _sb_kernel(q_ref, k_ref, v_ref, o_ref, acc_ref, after_ref, *, blk, scale):
    i = pl.program_id(2)
    q = q_ref[...]
    rows = lax.broadcasted_iota(jnp.int32, (blk, blk), 0)
    cols = lax.broadcasted_iota(jnp.int32, (blk, blk), 1)
    later = (rows > cols).astype(BF16)

    def block(c, diagonal):
        start = pl.multiple_of(c * blk, blk)
        kc = k_ref[pl.ds(start, blk), :]
        vc = v_ref[pl.ds(start, blk), :]
        z = _dot_nt(q, kc) * scale
        if diagonal:
            z = jnp.where(cols < rows, z, SB_MASK_VALUE)
        log_stay = _neg_softplus(z)
        hi, lo = _split_bf16(log_stay)
        within = _dot(hi, later) + _dot(lo, later)
        after = after_ref[...]
        w = jnp.exp(z + log_stay + within + after)
        acc_ref[...] += _dot(w.astype(BF16), vc)
        after_ref[...] = after + jnp.sum(log_stay, axis=-1, keepdims=True)

    acc_ref[...] = jnp.zeros_like(acc_ref)
    after_ref[...] = jnp.zeros_like(after_ref)
    block(i, True)

    def cond(c):
        return jnp.logical_and(c >= 0, jnp.max(after_ref[...]) > SB_LOG_UNDERFLOW)

    def body(c):
        block(c, False)
        return c - 1

    lax.while_loop(cond, body, i - 1)
    o_ref[...] = acc_ref[...].astype(o_ref.dtype)


def stick_breaking_attention(proj, *, batch, seq, blk):
    t = proj.shape[0]
    nq = seq // blk
    dh = SB_HEAD_DIM
    return pl.pallas_call(
        functools.partial(_sb_kernel, blk=blk, scale=dh ** -0.5),
        grid=(batch, SB_HEADS, nq),
        in_specs=[
            pl.BlockSpec((blk, dh), lambda b, h, i: (b * nq + i, h)),
            pl.BlockSpec((seq, dh), lambda b, h, i: (b, SB_HEADS + h)),
            pl.BlockSpec((seq, dh), lambda b, h, i: (b, 2 * SB_HEADS + h)),
        ],
        out_specs=pl.BlockSpec((blk, dh), lambda b, h, i: (b * nq + i, h)),
        out_shape=jax.ShapeDtypeStruct((t, SB_WIDTH), BF16),
        scratch_shapes=[pltpu.VMEM((blk, dh), F32), pltpu.VMEM((blk, 1), F32)],
        compiler_params=_params("parallel", "parallel", "arbitrary"),
        name="stick_breaking",
    )(proj, proj, proj)


def _pool_kernel(u_ref, halo_ref, pw_ref, ps_ref, o_ref, *, ts):
    i = pl.program_id(1)
    halo = halo_ref[...].astype(F32)
    halo = jnp.where(i > 0, halo, 0.0)
    pos = i * ts + lax.broadcasted_iota(jnp.int32, (ts, 1), 0)
    c = POOL_GROUP_DIM
    for gi, win in enumerate(POOL_WINDOWS):
        u = u_ref[:, gi * c:(gi + 1) * c].astype(F32)
        ext = jnp.concatenate([halo[:, gi * c:(gi + 1) * c], u], axis=0)
        s, span = ext, 1
        while span < win:
            s = s[span:] + s[:-span]
            span *= 2
        wsum = s[POOL_HALO - win + 1:]
        count = jnp.minimum(pos + 1, win).astype(F32)
        pooled = wsum / count - u
        y = _dot(pooled.astype(BF16), pw_ref[gi])
        o_ref[:, gi * c:(gi + 1) * c] = (y * ps_ref[:, gi * c:(gi + 1) * c]).astype(o_ref.dtype)


def multiscale_pool(proj, pool_w, pool_scale, *, batch, seq, ts):
    t = proj.shape[0]
    ns = seq // ts
    width = pool_scale.shape[0]
    ublk = (3 * SB_WIDTH) // width
    hb = ts // POOL_HALO
    return pl.pallas_call(
        functools.partial(_pool_kernel, ts=ts),
        grid=(batch, ns),
        in_specs=[
            pl.BlockSpec((ts, width), lambda b, i: (b * ns + i, ublk)),
            pl.BlockSpec((POOL_HALO, width),
                         lambda b, i: (jnp.maximum((b * ns + i) * hb - 1, 0), ublk)),
            pl.BlockSpec(pool_w.shape, lambda b, i: (0, 0, 0)),
            pl.BlockSpec((1, width), lambda b, i: (0, 0)),
        ],
        out_specs=pl.BlockSpec((ts, width), lambda b, i: (b * ns + i, 0)),
        out_shape=jax.ShapeDtypeStruct((t, width), BF16),
        compiler_params=_params("parallel", "parallel"),
        name="multiscale_pool",
    )(proj, proj, pool_w, pool_scale.reshape(1, width))


def _gla_kernel(q_ref, k_ref, v_ref, r_ref, a_ref, gw_ref, gb_ref, gn_ref, o_ref, state_ref,
                *, ts, chunk, scale):
    @pl.when(pl.program_id(2) == 0)
    def _():
        state_ref[...] = jnp.zeros_like(state_ref)

    dk = q_ref.shape[1]
    gate_in = _dot(a_ref[...], gw_ref[...]) + gb_ref[...]
    g = _neg_softplus(-gate_in) * (1.0 / GLA_GATE_NORMALIZER)
    li = lax.broadcasted_iota(jnp.int32, (chunk, chunk), 0)
    mi = lax.broadcasted_iota(jnp.int32, (chunk, chunk), 1)
    causal = mi <= li
    causal_bf = causal.astype(BF16)
    for c in range(ts // chunk):
        sl = slice(c * chunk, (c + 1) * chunk)
        g_hi, g_lo = _split_bf16(g[sl])
        b = _dot(causal_bf, g_hi) + _dot(causal_bf, g_lo)
        b_mid = b[chunk // 2:chunk // 2 + 1]
        b_last = b[chunk - 1:chunk]
        qc = q_ref[sl, :].astype(F32) * scale
        kc = k_ref[sl, :].astype(F32)
        vc = v_ref[sl, :]
        qi = (qc * jnp.exp(b - b_mid)).astype(BF16)
        ki = (kc * jnp.exp(b_mid - b)).astype(BF16)
        scores = jnp.where(causal, _dot_nt(qi, ki), 0.0).astype(BF16)
        state = state_ref[...]
        q_inter = (qc * jnp.exp(b)).astype(BF16)
        o = _dot(scores, vc) + _dot(q_inter, state.astype(BF16))
        k_state = (kc * jnp.exp(b_last - b)).astype(BF16)
        decay = jnp.broadcast_to(jnp.exp(b_last), (LANES, dk)).T[:, :1]
        state_ref[...] = decay * state + _dot_tn(k_state, vc)
        rc = r_ref[sl, :].astype(F32)
        o_ref[sl, :] = (_rms(o, gn_ref[...]) * (rc * jax.nn.sigmoid(rc))).astype(o_ref.dtype)


def gated_linear_attention(proj, a, gate_w, gate_b, gnorm, *, batch, seq, ts):
    t = proj.shape[0]
    ns = seq // ts
    dk, dv, nh = GLA_HK, GLA_HV, GLA_HEADS
    kw = nh * dk
    return pl.pallas_call(
        functools.partial(_gla_kernel, ts=ts, chunk=GLA_CHUNK, scale=dk ** -0.5),
        grid=(batch, nh, ns),
        in_specs=[
            pl.BlockSpec((ts, dk), lambda b, h, s: (b * ns + s, h)),
            pl.BlockSpec((ts, dk), lambda b, h, s: (b * ns + s, nh + h)),
            pl.BlockSpec((ts, dv), lambda b, h, s: (b * ns + s, (2 * kw) // dv + h)),
            pl.BlockSpec((ts, dv), lambda b, h, s: (b * ns + s, (2 * kw) // dv + nh + h)),
            pl.BlockSpec((ts, LANES), lambda b, h, s: (b * ns + s, 0)),
            pl.BlockSpec((LANES, dk), lambda b, h, s: (0, h)),
            pl.BlockSpec((1, dk), lambda b, h, s: (0, h)),
            pl.BlockSpec((1, dv), lambda b, h, s: (0, 0)),
        ],
        out_specs=pl.BlockSpec((ts, dv), lambda b, h, s: (b * ns + s, h)),
        out_shape=jax.ShapeDtypeStruct((t, nh * dv), BF16),
        scratch_shapes=[pltpu.VMEM((dk, dv), F32)],
        compiler_params=_params("parallel", "parallel", "arbitrary"),
        name="gated_linear_attention",
    )(proj, proj, proj, proj, a, gate_w, gate_b.reshape(1, kw), gnorm.reshape(1, dv))


def _xattn_kernel(q_ref, k_ref, v_ref, o_ref, *, heads, scale):
    dh = q_ref.shape[1] // heads
    for h in range(heads):
        sl = slice(h * dh, (h + 1) * dh)
        s = _dot_nt(q_ref[:, sl], k_ref[:, sl]) * scale
        p = jnp.exp(s - jnp.max(s, axis=-1, keepdims=True))
        den = jnp.sum(p, axis=-1, keepdims=True)
        o_ref[:, sl] = (_dot(p.astype(BF16), v_ref[:, sl]) / den).astype(o_ref.dtype)


def memory_cross_attention(q, kv, layer, *, batch, seq, n_mem, ts):
    t, d = q.shape
    ns = seq // ts
    return pl.pallas_call(
        functools.partial(_xattn_kernel, heads=X_HEADS, scale=(d // X_HEADS) ** -0.5),
        grid=(batch, ns),
        in_specs=[
            pl.BlockSpec((ts, d), lambda b, s: (b * ns + s, 0)),
            pl.BlockSpec((n_mem, d), lambda b, s: (b, 2 * layer)),
            pl.BlockSpec((n_mem, d), lambda b, s: (b, 2 * layer + 1)),
        ],
        out_specs=pl.BlockSpec((ts, d), lambda b, s: (b * ns + s, 0)),
        out_shape=jax.ShapeDtypeStruct((t, d), BF16),
        compiler_params=_params("parallel", "parallel"),
        name="memory_cross_attention",
    )(q, kv, kv)


def _ffn_kernel(x_ref, gpre_ref, wg_ref, wu_ref, wd_ref, gpost_ref, o_ref, h_ref, acc_ref):
    f = pl.program_id(1)

    @pl.when(f == 0)
    def _():
        h_ref[...] = _rms(x_ref[...], gpre_ref[...]).astype(BF16)
        acc_ref[...] = jnp.zeros_like(acc_ref)

    h = h_ref[...]
    gate = _dot(h, wg_ref[...])
    up = _dot(h, wu_ref[...])
    act = (gate * jax.nn.sigmoid(gate) * up).astype(BF16)
    acc_ref[...] += _dot(act, wd_ref[...])

    @pl.when(f == pl.num_programs(1) - 1)
    def _():
        o_ref[...] = x_ref[...] + _rms(acc_ref[...], gpost_ref[...])


def swiglu_sublayer(x, g_pre, w_gate, w_up, w_down, g_post, *, tm, tf):
    m, d = x.shape
    dff = w_gate.shape[1]
    return pl.pallas_call(
        _ffn_kernel,
        grid=(m // tm, dff // tf),
        in_specs=[
            pl.BlockSpec((tm, d), lambda i, f: (i, 0)),
            pl.BlockSpec((1, d), lambda i, f: (0, 0)),
            pl.BlockSpec((d, tf), lambda i, f: (0, f)),
            pl.BlockSpec((d, tf), lambda i, f: (0, f)),
            pl.BlockSpec((tf, d), lambda i, f: (f, 0)),
            pl.BlockSpec((1, d), lambda i, f: (0, 0)),
        ],
        out_specs=pl.BlockSpec((tm, d), lambda i, f: (i, 0)),
        out_shape=jax.ShapeDtypeStruct((m, d), F32),
        scratch_shapes=[pltpu.VMEM((tm, d), BF16), pltpu.VMEM((tm, d), F32)],
        compiler_params=_params("parallel", "arbitrary"),
        name="swiglu_sublayer",
    )(x, g_pre.reshape(1, d), w_gate, w_up, w_down, g_post.reshape(1, d))


def kernel(x, mem, norms, mem_norm, even_w_in, pool_w, pool_scale, even_w_out,
           odd_w_in, gla_gate_w, gla_gate_b, gla_gnorm, odd_w_out,
           xattn_wq, xattn_wk, xattn_wv, xattn_wo,
           ffn_w_gate, ffn_w_up, ffn_w_down):
    batch, seq, d = x.shape
    n_mem = mem.shape[1]
    depth = norms.shape[0]
    t = batch * seq
    bf = lambda w: w.astype(BF16)

    xs = x.reshape(t, d)
    w_kv = jnp.concatenate([bf(w) for l in range(depth) for w in (xattn_wk[l], xattn_wv[l])], axis=1)
    kv = norm_matmul(mem.reshape(batch * n_mem, d), mem_norm, w_kv, tm=batch * n_mem, tn=1024)

    for layer in range(depth):
        g = norms[layer]
        i = layer // 2
        if layer % 2 == 0:
            proj = norm_matmul(xs, g[0], bf(even_w_in[i]), tm=1024, tn=1024)
            o_sb = stick_breaking_attention(proj, batch=batch, seq=seq, blk=256)
            o_pool = multiscale_pool(proj, bf(pool_w[i]), pool_scale[i], batch=batch, seq=seq, ts=512)
            w_out = bf(even_w_out[i])
            xs = matmul_norm_res([o_sb, o_pool], [w_out[:SB_WIDTH], w_out[SB_WIDTH:]], g[1], xs, tm=512)
        else:
            n_main = 2 * GLA_HEADS * GLA_HK + 2 * GLA_HEADS * GLA_HV
            w_in = bf(odd_w_in[i])
            w_a = jnp.pad(w_in[:, n_main:], ((0, 0), (0, LANES - GLA_GATE_RANK)))
            gate_w = jnp.pad(bf(gla_gate_w[i]), ((0, LANES - GLA_GATE_RANK), (0, 0)))
            proj, a = norm_matmul(xs, g[0], w_in[:, :n_main], w_a, tm=1024, tn=1024)
            o = gated_linear_attention(proj, a, gate_w, gla_gate_b[i], gla_gnorm[i],
                                       batch=batch, seq=seq, ts=512)
            xs = matmul_norm_res([o], [bf(odd_w_out[i])], g[1], xs, tm=512)
        q = norm_matmul(xs, g[2], bf(xattn_wq[layer]), tm=1024, tn=1024)
        o = memory_cross_attention(q, kv, layer, batch=batch, seq=seq, n_mem=n_mem, ts=512)
        xs = matmul_norm_res([o], [bf(xattn_wo[layer])], g[3], xs, tm=512)
        xs = swiglu_sublayer(xs, g[4], bf(ffn_w_gate[layer]), bf(ffn_w_up[layer]),
                             bf(ffn_w_down[layer]), g[5], tm=512, tf=512)
    return xs.reshape(batch, seq, d)
```

```python
import functools

import jax
import jax.numpy as jnp
from jax import lax
from jax.experimental import pallas as pl
from jax.experimental.pallas import tpu as pltpu

F32 = jnp.float32
BF16 = jnp.bfloat16

EPS = 1e-6

SB_HEADS = 4
SB_HEAD_DIM = 256
SB_WIDTH = SB_HEADS * SB_HEAD_DIM
SB_MASK_VALUE = -1e4
SB_LOG_UNDERFLOW = -88.0
POOL_WINDOWS = (2, 4, 8, 16)
POOL_HALO = 16
POOL_GROUP_DIM = 256

GLA_HEADS = 4
GLA_HK = 256
GLA_HV = 512
GLA_GATE_RANK = 16
GLA_GATE_NORMALIZER = 16.0
GLA_CHUNK = 64
GLA_SUBCHUNKS = 2
GLA_HEADS_PER_STEP = 2

X_HEADS = 4

V7X_VMEM_LIMIT_BYTES = 56 * 1024 * 1024
LANES = 128
SUB_ROWS = 256


def _params(*sem):
    return pltpu.CompilerParams(dimension_semantics=sem, vmem_limit_bytes=V7X_VMEM_LIMIT_BYTES)


def _resident(shape):
    zeros = (0,) * len(shape)
    return pl.BlockSpec(shape, lambda *_: zeros, pipeline_mode=pl.Buffered(1))


def _rms(x, g):
    ms = jnp.mean(x * x, axis=-1, keepdims=True)
    return x * lax.rsqrt(ms + EPS) * g


def _dot(a, b):
    return jnp.dot(a, b, preferred_element_type=F32)


def _dot_nt(a, b):
    return lax.dot_general(a, b, (((1,), (1,)), ((), ())), preferred_element_type=F32)


def _dot_tn(a, b):
    return lax.dot_general(a, b, (((0,), (0,)), ((), ())), preferred_element_type=F32)


def _split_bf16(x):
    hi = x.astype(BF16)
    lo = (x - hi.astype(F32)).astype(BF16)
    return hi, lo


def _log1p_exp_neg_abs(x):
    return jnp.log(1.0 + jnp.exp(-jnp.abs(x)))


def _silu(x):
    return x * jax.nn.sigmoid(x)


def _norm_matmul_kernel(x_ref, g_ref, w_ref, *rest, has_extra, sub):
    if has_extra:
        w2_ref, o_ref, o2_ref = rest
    else:
        (o_ref,) = rest
    for r in range(x_ref.shape[0] // sub):
        rows = slice(r * sub, (r + 1) * sub)
        h = _rms(x_ref[rows, :], g_ref[...]).astype(BF16)
        o_ref[rows, :] = _dot(h, w_ref[...]).astype(o_ref.dtype)
        if has_extra:
            o2_ref[rows, :] = _dot(h, w2_ref[...]).astype(o2_ref.dtype)


def norm_matmul(x, g, w, w_extra=None, *, tm):
    m, k = x.shape
    n = w.shape[1]
    has_extra = w_extra is not None
    in_specs = [pl.BlockSpec((tm, k), lambda i: (i, 0)), _resident((1, k)), _resident((k, n))]
    out_specs = [pl.BlockSpec((tm, n), lambda i: (i, 0))]
    out_shape = [jax.ShapeDtypeStruct((m, n), BF16)]
    args = [x, g.reshape(1, k), w]
    if has_extra:
        n2 = w_extra.shape[1]
        in_specs.append(_resident((k, n2)))
        out_specs.append(pl.BlockSpec((tm, n2), lambda i: (i, 0)))
        out_shape.append(jax.ShapeDtypeStruct((m, n2), BF16))
        args.append(w_extra)
    out = pl.pallas_call(
        functools.partial(_norm_matmul_kernel, has_extra=has_extra, sub=min(SUB_ROWS, tm)),
        grid=(m // tm,),
        in_specs=in_specs,
        out_specs=out_specs,
        out_shape=out_shape,
        compiler_params=_params("parallel"),
        name="norm_matmul",
    )(*args)
    return out if has_extra else out[0]


def _matmul_norm_res_kernel(*refs, n_in, sub):
    a_refs = refs[:n_in]
    w_ref, g_ref, x_ref, o_ref = refs[n_in:]
    for r in range(x_ref.shape[0] // sub):
        rows = slice(r * sub, (r + 1) * sub)
        y, k0 = None, 0
        for a_ref in a_refs:
            k1 = k0 + a_ref.shape[1]
            part = _dot(a_ref[rows, :], w_ref[k0:k1, :])
            y = part if y is None else y + part
            k0 = k1
        o_ref[rows, :] = x_ref[rows, :] + _rms(y, g_ref[...])


def _gated_matmul_norm_res_kernel(o_ref, r_ref, gn_ref, w_ref, g_ref, x_ref, out_ref, *, heads, sub):
    dv = o_ref.shape[1] // heads
    for r in range(x_ref.shape[0] // sub):
        rows = slice(r * sub, (r + 1) * sub)
        y = None
        for hd in range(heads):
            sl = slice(hd * dv, (hd + 1) * dv)
            a = _rms(o_ref[rows, sl].astype(F32), gn_ref[...]) * _silu(r_ref[rows, sl].astype(F32))
            part = _dot(a.astype(BF16), w_ref[sl, :])
            y = part if y is None else y + part
        out_ref[rows, :] = x_ref[rows, :] + _rms(y, g_ref[...])


def gated_matmul_norm_res(o, proj, r_block, gnorm, w, g, x, *, heads, tm):
    m, n = x.shape
    kdim = o.shape[1]
    return pl.pallas_call(
        functools.partial(_gated_matmul_norm_res_kernel, heads=heads, sub=min(SUB_ROWS, tm)),
        grid=(m // tm,),
        in_specs=[
            pl.BlockSpec((tm, kdim), lambda i: (i, 0)),
            pl.BlockSpec((tm, kdim), lambda i: (i, r_block)),
            _resident((1, kdim // heads)),
            _resident(w.shape),
            _resident((1, n)),
            pl.BlockSpec((tm, n), lambda i: (i, 0)),
        ],
        out_specs=pl.BlockSpec((tm, n), lambda i: (i, 0)),
        out_shape=jax.ShapeDtypeStruct((m, n), F32),
        compiler_params=_params("parallel"),
        name="gated_matmul_norm_res",
    )(o, proj, gnorm.reshape(1, kdim // heads), w, g.reshape(1, n), x)


def matmul_norm_res(a_list, w, g, x, *, tm):
    m, n = x.shape
    in_specs = [pl.BlockSpec((tm, a.shape[1]), lambda i: (i, 0)) for a in a_list]
    in_specs += [_resident(w.shape), _resident((1, n)), pl.BlockSpec((tm, n), lambda i: (i, 0))]
    return pl.pallas_call(
        functools.partial(_matmul_norm_res_kernel, n_in=len(a_list), sub=min(SUB_ROWS, tm)),
        grid=(m // tm,),
        in_specs=in_specs,
        out_specs=pl.BlockSpec((tm, n), lambda i: (i, 0)),
        out_shape=jax.ShapeDtypeStruct((m, n), F32),
        compiler_params=_params("parallel"),
        name="matmul_norm_res",
    )(*a_list, w, g.reshape(1, n), x)


def _sb_kernel(q_ref, k_ref, v_ref, o_ref, acc_ref, after_ref, *, blk, nsub, scale):
    i = pl.program_id(2)
    rows = lax.broadcasted_iota(jnp.int32, (blk, blk), 0)
    cols = lax.broadcasted_iota(jnp.int32, (blk, blk), 1)
    later = (rows > cols).astype(BF16)
    later2 = jnp.concatenate([later, later], axis=0)

    def scaled_q(j):
        return (q_ref[j * blk:(j + 1) * blk, :].astype(F32) * scale).astype(BF16)

    def block(q, c, after, kind):
        start = pl.multiple_of(jnp.maximum(c, 0) * blk, blk)
        kc = k_ref[pl.ds(start, blk), :]
        vc = v_ref[pl.ds(start, blk), :]
        z = _dot_nt(q, kc)
        if kind == "diagonal":
            z = jnp.where(cols < rows, z, SB_MASK_VALUE)
        elif kind == "previous":
            z = jnp.where(c >= 0, z, SB_MASK_VALUE)
        smooth = _log1p_exp_neg_abs(z)
        log_stay = jnp.minimum(-z, 0.0) - smooth
        log_beta = jnp.minimum(z, 0.0) - smooth
        hi, lo = _split_bf16(log_stay)
        within = _dot(jnp.concatenate([hi, lo], axis=1), later2)
        w = jnp.exp(log_beta + within + after)
        return _dot(w.astype(BF16), vc), after + jnp.sum(log_stay, axis=-1, keepdims=True)

    for j in range(nsub):
        c = i * nsub + j
        q = scaled_q(j)
        acc_d, after_d = block(q, c, jnp.zeros((blk, 1), F32), "diagonal")
        acc_p, after_p = block(q, c - 1, after_d, "previous")
        acc_ref[j] = acc_d + acc_p
        after_ref[j] = after_p

    for j in range(nsub):
        q = scaled_q(j)

        def cond(c, j=j):
            return jnp.logical_and(c >= 0, jnp.max(after_ref[j]) > SB_LOG_UNDERFLOW)

        def body(c, j=j, q=q):
            acc_c, after_c = block(q, c, after_ref[j], "tail")
            acc_ref[j] += acc_c
            after_ref[j] = after_c
            return c - 1

        lax.while_loop(cond, body, i * nsub + j - 2)
        o_ref[j * blk:(j + 1) * blk, :] = acc_ref[j].astype(o_ref.dtype)


def stick_breaking_attention(proj, *, batch, seq, blk, nsub):
    t = proj.shape[0]
    tq = blk * nsub
    nq = seq // tq
    dh = SB_HEAD_DIM
    return pl.pallas_call(
        functools.partial(_sb_kernel, blk=blk, nsub=nsub, scale=dh ** -0.5),
        grid=(batch, SB_HEADS, nq),
        in_specs=[
            pl.BlockSpec((tq, dh), lambda b, h, i: (b * nq + i, h)),
            pl.BlockSpec((seq, dh), lambda b, h, i: (b, SB_HEADS + h)),
            pl.BlockSpec((seq, dh), lambda b, h, i: (b, 2 * SB_HEADS + h)),
        ],
        out_specs=pl.BlockSpec((tq, dh), lambda b, h, i: (b * nq + i, h)),
        out_shape=jax.ShapeDtypeStruct((t, SB_WIDTH), BF16),
        scratch_shapes=[pltpu.VMEM((nsub, blk, dh), F32), pltpu.VMEM((nsub, blk, 1), F32)],
        compiler_params=_params("parallel", "parallel", "arbitrary"),
        name="stick_breaking",
    )(proj, proj, proj)


def _pool_kernel(u_ref, halo_ref, pw_ref, ps_ref, o_ref, *, ts):
    i = pl.program_id(1)
    halo = halo_ref[...].astype(F32)
    halo = jnp.where(i > 0, halo, 0.0)
    pos = i * ts + lax.broadcasted_iota(jnp.int32, (ts, 1), 0)
    c = POOL_GROUP_DIM
    for gi, win in enumerate(POOL_WINDOWS):
        u = u_ref[:, gi * c:(gi + 1) * c].astype(F32)
        ext = jnp.concatenate([halo[:, gi * c:(gi + 1) * c], u], axis=0)
        s, span = ext, 1
        while span < win:
            s = s[span:] + s[:-span]
            span *= 2
        wsum = s[POOL_HALO - win + 1:]
        count = jnp.minimum(pos + 1, win).astype(F32)
        pooled = wsum / count - u
        y = _dot(pooled.astype(BF16), pw_ref[gi])
        o_ref[:, gi * c:(gi + 1) * c] = (y * ps_ref[:, gi * c:(gi + 1) * c]).astype(o_ref.dtype)


def multiscale_pool(proj, pool_w, pool_scale, *, batch, seq, ts):
    t = proj.shape[0]
    ns = seq // ts
    width = pool_scale.shape[0]
    ublk = (3 * SB_WIDTH) // width
    hb = ts // POOL_HALO
    return pl.pallas_call(
        functools.partial(_pool_kernel, ts=ts),
        grid=(batch, ns),
        in_specs=[
            pl.BlockSpec((ts, width), lambda b, i: (b * ns + i, ublk)),
            pl.BlockSpec((POOL_HALO, width),
                         lambda b, i: (jnp.maximum((b * ns + i) * hb - 1, 0), ublk)),
            pl.BlockSpec(pool_w.shape, lambda b, i: (0, 0, 0)),
            pl.BlockSpec((1, width), lambda b, i: (0, 0)),
        ],
        out_specs=pl.BlockSpec((ts, width), lambda b, i: (b * ns + i, 0)),
        out_shape=jax.ShapeDtypeStruct((t, width), BF16),
        compiler_params=_params("parallel", "parallel"),
        name="multiscale_pool",
    )(proj, proj, pool_w, pool_scale.reshape(1, width))


def _gla_kernel(q_ref, k_ref, v_ref, a_ref, gw_ref, gb_ref, o_ref, state_ref,
                *, ts, chunk, nsc, dk, dv, scale):
    @pl.when(pl.program_id(2) == 0)
    def _():
        state_ref[...] = jnp.zeros_like(state_ref)

    heads = state_ref.shape[0]
    group = chunk * nsc
    gate_in = _dot(a_ref[...], gw_ref[...]) + gb_ref[...]
    g = (jnp.minimum(gate_in, 0.0) - _log1p_exp_neg_abs(gate_in)) * (1.0 / GLA_GATE_NORMALIZER)
    li = lax.broadcasted_iota(jnp.int32, (chunk, chunk), 0)
    mi = lax.broadcasted_iota(jnp.int32, (chunk, chunk), 1)
    causal = mi <= li
    causal_bf = causal.astype(BF16)
    causal2 = jnp.concatenate([causal_bf, causal_bf], axis=1)
    for c in range(ts // group):
        for hh in range(heads):
            kcols = slice(hh * dk, (hh + 1) * dk)
            vcols = slice(hh * dv, (hh + 1) * dv)
            state = state_ref[hh]
            q_loc, k_loc, k_loc_bf, b_end, v_sub, o_sub = [], [], [], [], [], []
            for si in range(nsc):
                sl = slice(c * group + si * chunk, c * group + (si + 1) * chunk)
                g_hi, g_lo = _split_bf16(g[sl, kcols])
                b = _dot(causal2, jnp.concatenate([g_hi, g_lo], axis=0))
                b_mid = b[chunk // 2:chunk // 2 + 1]
                b_last = b[chunk - 1:chunk]
                qc = q_ref[sl, kcols].astype(F32) * scale
                kc = k_ref[sl, kcols].astype(F32)
                vc = v_ref[sl, vcols]
                qi = (qc * jnp.exp(b - b_mid)).astype(BF16)
                ki = (kc * jnp.exp(b_mid - b)).astype(BF16)
                scores = jnp.where(causal, _dot_nt(qi, ki), 0.0).astype(BF16)
                o_sub.append(_dot(scores, vc))
                q_loc.append(qc * jnp.exp(b))
                k_loc.append(kc * jnp.exp(b_last - b))
                k_loc_bf.append(k_loc[si].astype(BF16))
                b_end.append(b_last)
                v_sub.append(vc)
            prefix = [None]
            for si in range(nsc):
                prefix.append(b_end[si] if prefix[si] is None else prefix[si] + b_end[si])
            total = prefix[nsc]
            q_inter, k_state = [], []
            for si in range(nsc):
                q_inter.append((q_loc[si] if si == 0 else q_loc[si] * jnp.exp(prefix[si])).astype(BF16))
                k_state.append(k_loc_bf[si] if si == nsc - 1
                               else (k_loc[si] * jnp.exp(total - prefix[si + 1])).astype(BF16))
                for sj in range(si):
                    qs = (q_loc[si] if sj == si - 1
                          else q_loc[si] * jnp.exp(prefix[si] - prefix[sj + 1])).astype(BF16)
                    cross = _dot_nt(qs, k_loc_bf[sj]).astype(BF16)
                    o_sub[si] = o_sub[si] + _dot(cross, v_sub[sj])
            inter = _dot(jnp.concatenate(q_inter, axis=0), state.astype(BF16))
            decay = jnp.broadcast_to(jnp.exp(total), (LANES, dk)).T[:, :1]
            state_ref[hh] = decay * state + _dot_tn(jnp.concatenate(k_state, axis=0),
                                                     jnp.concatenate(v_sub, axis=0))
            for si in range(nsc):
                sl = slice(c * group + si * chunk, c * group + (si + 1) * chunk)
                o_ref[sl, vcols] = (o_sub[si] + inter[si * chunk:(si + 1) * chunk]).astype(o_ref.dtype)


def gated_linear_attention(proj, a, gate_w, gate_b, *, batch, seq, ts):
    t = proj.shape[0]
    ns = seq // ts
    dk, dv, nh, hps = GLA_HK, GLA_HV, GLA_HEADS, GLA_HEADS_PER_STEP
    kw, vw = nh * dk, nh * dv
    ng = nh // hps
    bk, bv = hps * dk, hps * dv
    return pl.pallas_call(
        functools.partial(_gla_kernel, ts=ts, chunk=GLA_CHUNK, nsc=GLA_SUBCHUNKS, dk=dk, dv=dv,
                          scale=dk ** -0.5),
        grid=(batch, ng, ns),
        in_specs=[
            pl.BlockSpec((ts, bk), lambda b, h, s: (b * ns + s, h)),
            pl.BlockSpec((ts, bk), lambda b, h, s: (b * ns + s, ng + h)),
            pl.BlockSpec((ts, bv), lambda b, h, s: (b * ns + s, (2 * kw) // bv + h)),
            pl.BlockSpec((ts, LANES), lambda b, h, s: (b * ns + s, 0)),
            pl.BlockSpec((LANES, bk), lambda b, h, s: (0, h)),
            pl.BlockSpec((1, bk), lambda b, h, s: (0, h)),
        ],
        out_specs=pl.BlockSpec((ts, bv), lambda b, h, s: (b * ns + s, h)),
        out_shape=jax.ShapeDtypeStruct((t, vw), BF16),
        scratch_shapes=[pltpu.VMEM((hps, dk, dv), F32)],
        compiler_params=_params("parallel", "parallel", "arbitrary"),
        name="gated_linear_attention",
    )(proj, proj, proj, a, gate_w, gate_b.reshape(1, kw))


def _xattn_sublayer_kernel(x_ref, gpre_ref, wq_ref, k_ref, v_ref, wo_ref, gpost_ref, o_ref,
                           *, heads, scale, sub):
    dh = wq_ref.shape[1] // heads
    for r in range(x_ref.shape[0] // sub):
        rows = slice(r * sub, (r + 1) * sub)
        xr = x_ref[rows, :]
        h = _rms(xr, gpre_ref[...]).astype(BF16)
        q = _dot(h, wq_ref[...]).astype(BF16)
        y = None
        for hd in range(heads):
            sl = slice(hd * dh, (hd + 1) * dh)
            s = _dot_nt(q[:, sl], k_ref[:, sl]) * scale
            p = jnp.exp(s - jnp.max(s, axis=-1, keepdims=True))
            den = jnp.sum(p, axis=-1, keepdims=True)
            o_h = (_dot(p.astype(BF16), v_ref[:, sl]) / den).astype(BF16)
            part = _dot(o_h, wo_ref[sl, :])
            y = part if y is None else y + part
        o_ref[rows, :] = xr + _rms(y, gpost_ref[...])


def cross_attention_sublayer(x, g_pre, wq, kv, layer, wo, g_post, *, batch, seq, n_mem, tm):
    t, d = x.shape
    ns = seq // tm
    return pl.pallas_call(
        functools.partial(_xattn_sublayer_kernel, heads=X_HEADS, scale=(d // X_HEADS) ** -0.5,
                          sub=min(SUB_ROWS, tm)),
        grid=(batch, ns),
        in_specs=[
            pl.BlockSpec((tm, d), lambda b, s: (b * ns + s, 0)),
            _resident((1, d)),
            _resident(wq.shape),
            pl.BlockSpec((n_mem, d), lambda b, s: (b, 2 * layer)),
            pl.BlockSpec((n_mem, d), lambda b, s: (b, 2 * layer + 1)),
            _resident(wo.shape),
            _resident((1, d)),
        ],
        out_specs=pl.BlockSpec((tm, d), lambda b, s: (b * ns + s, 0)),
        out_shape=jax.ShapeDtypeStruct((t, d), F32),
        compiler_params=_params("parallel", "parallel"),
        name="cross_attention_sublayer",
    )(x, g_pre.reshape(1, d), wq, kv, kv, wo, g_post.reshape(1, d))


def _ffn_kernel(x_ref, gpre_ref, wg_ref, wu_ref, wd_ref, gpost_ref, o_ref, h_ref, acc_ref, *, sub):
    f = pl.program_id(1)
    last = pl.num_programs(1) - 1
    tm = x_ref.shape[0]

    def step(first, final, rows_per):
        for r in range(tm // rows_per):
            rows = slice(r * rows_per, (r + 1) * rows_per)
            if first:
                h = _rms(x_ref[rows, :], gpre_ref[...]).astype(BF16)
                h_ref[rows, :] = h
            else:
                h = h_ref[rows, :]
            act = (_silu(_dot(h, wg_ref[...])) * _dot(h, wu_ref[...])).astype(BF16)
            y = _dot(act, wd_ref[...])
            if not first:
                y = acc_ref[rows, :] + y
            if final:
                o_ref[rows, :] = x_ref[rows, :] + _rms(y, gpost_ref[...])
            else:
                acc_ref[rows, :] = y

    pl.when(f == 0)(lambda: step(True, False, sub))
    pl.when(jnp.logical_and(f > 0, f < last))(lambda: step(False, False, tm))
    pl.when(f == last)(lambda: step(False, True, sub))


def swiglu_sublayer(x, g_pre, w_gate, w_up, w_down, g_post, *, tm, tf):
    m, d = x.shape
    dff = w_gate.shape[1]
    assert dff // tf >= 2
    return pl.pallas_call(
        functools.partial(_ffn_kernel, sub=min(SUB_ROWS, tm)),
        grid=(m // tm, dff // tf),
        in_specs=[
            pl.BlockSpec((tm, d), lambda i, f: (i, 0)),
            _resident((1, d)),
            pl.BlockSpec((d, tf), lambda i, f: (0, f)),
            pl.BlockSpec((d, tf), lambda i, f: (0, f)),
            pl.BlockSpec((tf, d), lambda i, f: (f, 0)),
            _resident((1, d)),
        ],
        out_specs=pl.BlockSpec((tm, d), lambda i, f: (i, 0)),
        out_shape=jax.ShapeDtypeStruct((m, d), F32),
        scratch_shapes=[pltpu.VMEM((tm, d), BF16), pltpu.VMEM((tm, d), F32)],
        compiler_params=_params("parallel", "arbitrary"),
        name="swiglu_sublayer",
    )(x, g_pre.reshape(1, d), w_gate, w_up, w_down, g_post.reshape(1, d))


def kernel(x, mem, norms, mem_norm, even_w_in, pool_w, pool_scale, even_w_out,
           odd_w_in, gla_gate_w, gla_gate_b, gla_gnorm, odd_w_out,
           xattn_wq, xattn_wk, xattn_wv, xattn_wo,
           ffn_w_gate, ffn_w_up, ffn_w_down):
    batch, seq, d = x.shape
    n_mem = mem.shape[1]
    depth = norms.shape[0]
    t = batch * seq
    bf = lambda w: w.astype(BF16)

    xs = x.reshape(t, d)
    w_kv = jnp.concatenate([bf(w) for l in range(depth) for w in (xattn_wk[l], xattn_wv[l])], axis=1)
    kv = norm_matmul(mem.reshape(batch * n_mem, d), mem_norm, w_kv, tm=SUB_ROWS)

    for layer in range(depth):
        g = norms[layer]
        i = layer // 2
        if layer % 2 == 0:
            proj = norm_matmul(xs, g[0], bf(even_w_in[i]), tm=1024)
            o_sb = stick_breaking_attention(proj, batch=batch, seq=seq, blk=256, nsub=4)
            o_pool = multiscale_pool(proj, bf(pool_w[i]), pool_scale[i], batch=batch, seq=seq, ts=512)
            xs = matmul_norm_res([o_sb, o_pool], bf(even_w_out[i]), g[1], xs, tm=1024)
        else:
            n_main = 2 * GLA_HEADS * GLA_HK + 2 * GLA_HEADS * GLA_HV
            w_a = jnp.pad(bf(odd_w_in[i][:, n_main:]), ((0, 0), (0, LANES - GLA_GATE_RANK)))
            gate_w = jnp.pad(bf(gla_gate_w[i]), ((0, LANES - GLA_GATE_RANK), (0, 0)))
            proj, a = norm_matmul(xs, g[0], bf(odd_w_in[i][:, :n_main]), w_a, tm=512)
            o = gated_linear_attention(proj, a, gate_w, gla_gate_b[i], batch=batch, seq=seq, ts=512)
            xs = gated_matmul_norm_res(o, proj, n_main // o.shape[1] - 1, gla_gnorm[i],
                                       bf(odd_w_out[i]), g[1], xs, heads=GLA_HEADS, tm=512)
        xs = cross_attention_sublayer(xs, g[2], bf(xattn_wq[layer]), kv, layer, bf(xattn_wo[layer]),
                                      g[3], batch=batch, seq=seq, n_mem=n_mem, tm=512)
        xs = swiglu_sublayer(xs, g[4], bf(ffn_w_gate[layer]), bf(ffn_w_up[layer]),
                             bf(ffn_w_down[layer]), g[5], tm=512, tf=512)
    return xs.reshape(batch, seq, d)
```

```python
import functools

import jax
import jax.numpy as jnp
from jax import lax
from jax.experimental import pallas as pl
from jax.experimental.pallas import tpu as pltpu

F32 = jnp.float32
BF16 = jnp.bfloat16

EPS = 1e-6

SB_HEADS = 4
SB_HEAD_DIM = 256
SB_WIDTH = SB_HEADS * SB_HEAD_DIM
SB_MASK_VALUE = -1e4
SB_LOG_UNDERFLOW = -88.0
POOL_WINDOWS = (2, 4, 8, 16)
POOL_HALO = 16
POOL_GROUP_DIM = 256

GLA_HEADS = 4
GLA_HK = 256
GLA_HV = 512
GLA_GATE_RANK = 16
GLA_GATE_NORMALIZER = 16.0
GLA_CHUNK = 64
GLA_SUBCHUNKS = 2
GLA_HEADS_PER_STEP = 2

X_HEADS = 4
FFN_HIDDEN_TILE = 512

V7X_VMEM_LIMIT_BYTES = 56 * 1024 * 1024
LANES = 128
SUB_ROWS = 256


def _params(*sem):
    return pltpu.CompilerParams(dimension_semantics=sem, vmem_limit_bytes=V7X_VMEM_LIMIT_BYTES)


def _resident(shape):
    zeros = (0,) * len(shape)
    return pl.BlockSpec(shape, lambda *_: zeros, pipeline_mode=pl.Buffered(1))


def _resident_layer(shape, layer):
    index = (layer,) + (0,) * (len(shape) - 1)
    return pl.BlockSpec((None,) + tuple(shape[1:]), lambda *_: index, pipeline_mode=pl.Buffered(1))


def _rms(x, g):
    ms = jnp.mean(x * x, axis=-1, keepdims=True)
    return x * lax.rsqrt(ms + EPS) * g


def _dot(a, b):
    return jnp.dot(a, b, preferred_element_type=F32)


def _dot_nt(a, b):
    return lax.dot_general(a, b, (((1,), (1,)), ((), ())), preferred_element_type=F32)


def _dot_tn(a, b):
    return lax.dot_general(a, b, (((0,), (0,)), ((), ())), preferred_element_type=F32)


def _split_bf16(x):
    hi = x.astype(BF16)
    lo = (x - hi.astype(F32)).astype(BF16)
    return hi, lo


def _log1p_exp_neg_abs(x):
    return jnp.log(1.0 + jnp.exp(-jnp.abs(x)))


def _silu(x):
    return x * jax.nn.sigmoid(x)


def _norm_matmul_kernel(x_ref, g_ref, w_ref, *rest, has_extra, sub):
    if has_extra:
        w2_ref, o_ref, o2_ref = rest
    else:
        (o_ref,) = rest
    for r in range(x_ref.shape[0] // sub):
        rows = slice(r * sub, (r + 1) * sub)
        h = _rms(x_ref[rows, :], g_ref[...]).astype(BF16)
        o_ref[rows, :] = _dot(h, w_ref[...]).astype(o_ref.dtype)
        if has_extra:
            o2_ref[rows, :] = _dot(h, w2_ref[...]).astype(o2_ref.dtype)


def norm_matmul(x, g, w, w_extra=None, *, tm):
    m, k = x.shape
    n = w.shape[1]
    has_extra = w_extra is not None
    in_specs = [pl.BlockSpec((tm, k), lambda i: (i, 0)), _resident((1, k)), _resident((k, n))]
    out_specs = [pl.BlockSpec((tm, n), lambda i: (i, 0))]
    out_shape = [jax.ShapeDtypeStruct((m, n), BF16)]
    args = [x, g.reshape(1, k), w]
    if has_extra:
        n2 = w_extra.shape[1]
        in_specs.append(_resident((k, n2)))
        out_specs.append(pl.BlockSpec((tm, n2), lambda i: (i, 0)))
        out_shape.append(jax.ShapeDtypeStruct((m, n2), BF16))
        args.append(w_extra)
    out = pl.pallas_call(
        functools.partial(_norm_matmul_kernel, has_extra=has_extra, sub=min(SUB_ROWS, tm)),
        grid=(m // tm,),
        in_specs=in_specs,
        out_specs=out_specs,
        out_shape=out_shape,
        compiler_params=_params("parallel"),
        name="norm_matmul",
    )(*args)
    return out if has_extra else out[0]


def _matmul_norm_res_kernel(*refs, n_in, sub):
    a_refs = refs[:n_in]
    w_ref, g_ref, x_ref, o_ref = refs[n_in:]
    for r in range(x_ref.shape[0] // sub):
        rows = slice(r * sub, (r + 1) * sub)
        y, k0 = None, 0
        for a_ref in a_refs:
            k1 = k0 + a_ref.shape[1]
            part = _dot(a_ref[rows, :], w_ref[k0:k1, :])
            y = part if y is None else y + part
            k0 = k1
        o_ref[rows, :] = x_ref[rows, :] + _rms(y, g_ref[...])


def _gated_matmul_norm_res_kernel(o_ref, r_ref, gn_ref, w_ref, g_ref, x_ref, out_ref, *, heads, sub):
    dv = o_ref.shape[1] // heads
    for r in range(x_ref.shape[0] // sub):
        rows = slice(r * sub, (r + 1) * sub)
        y = None
        for hd in range(heads):
            sl = slice(hd * dv, (hd + 1) * dv)
            a = _rms(o_ref[rows, sl].astype(F32), gn_ref[...]) * _silu(r_ref[rows, sl].astype(F32))
            part = _dot(a.astype(BF16), w_ref[sl, :])
            y = part if y is None else y + part
        out_ref[rows, :] = x_ref[rows, :] + _rms(y, g_ref[...])


def gated_matmul_norm_res(o, proj, r_block, gnorm, w, g, x, *, heads, tm):
    m, n = x.shape
    kdim = o.shape[1]
    return pl.pallas_call(
        functools.partial(_gated_matmul_norm_res_kernel, heads=heads, sub=min(SUB_ROWS, tm)),
        grid=(m // tm,),
        in_specs=[
            pl.BlockSpec((tm, kdim), lambda i: (i, 0)),
            pl.BlockSpec((tm, kdim), lambda i: (i, r_block)),
            _resident((1, kdim // heads)),
            _resident(w.shape),
            _resident((1, n)),
            pl.BlockSpec((tm, n), lambda i: (i, 0)),
        ],
        out_specs=pl.BlockSpec((tm, n), lambda i: (i, 0)),
        out_shape=jax.ShapeDtypeStruct((m, n), F32),
        compiler_params=_params("parallel"),
        name="gated_matmul_norm_res",
    )(o, proj, gnorm.reshape(1, kdim // heads), w, g.reshape(1, n), x)


def matmul_norm_res(a_list, w, g, x, *, tm):
    m, n = x.shape
    in_specs = [pl.BlockSpec((tm, a.shape[1]), lambda i: (i, 0)) for a in a_list]
    in_specs += [_resident(w.shape), _resident((1, n)), pl.BlockSpec((tm, n), lambda i: (i, 0))]
    return pl.pallas_call(
        functools.partial(_matmul_norm_res_kernel, n_in=len(a_list), sub=min(SUB_ROWS, tm)),
        grid=(m // tm,),
        in_specs=in_specs,
        out_specs=pl.BlockSpec((tm, n), lambda i: (i, 0)),
        out_shape=jax.ShapeDtypeStruct((m, n), F32),
        compiler_params=_params("parallel"),
        name="matmul_norm_res",
    )(*a_list, w, g.reshape(1, n), x)


def _sb_kernel(q_ref, k_ref, v_ref, o_ref, acc_ref, after_ref, *, blk, nsub, scale):
    i = pl.program_id(2)
    rows = lax.broadcasted_iota(jnp.int32, (blk, blk), 0)
    cols = lax.broadcasted_iota(jnp.int32, (blk, blk), 1)
    later = (rows > cols).astype(BF16)
    later2 = jnp.concatenate([later, later], axis=0)

    def scaled_q(j):
        return (q_ref[j * blk:(j + 1) * blk, :].astype(F32) * scale).astype(BF16)

    def block(q, c, after, kind):
        start = pl.multiple_of(jnp.maximum(c, 0) * blk, blk)
        kc = k_ref[pl.ds(start, blk), :]
        vc = v_ref[pl.ds(start, blk), :]
        z = _dot_nt(q, kc)
        if kind == "diagonal":
            z = jnp.where(cols < rows, z, SB_MASK_VALUE)
        elif kind == "previous":
            z = jnp.where(c >= 0, z, SB_MASK_VALUE)
        smooth = _log1p_exp_neg_abs(z)
        log_stay = jnp.minimum(-z, 0.0) - smooth
        log_beta = jnp.minimum(z, 0.0) - smooth
        hi, lo = _split_bf16(log_stay)
        within = _dot(jnp.concatenate([hi, lo], axis=1), later2)
        w = jnp.exp(log_beta + within + after)
        return _dot(w.astype(BF16), vc), after + jnp.sum(log_stay, axis=-1, keepdims=True)

    for j in range(nsub):
        c = i * nsub + j
        q = scaled_q(j)
        acc_d, after_d = block(q, c, jnp.zeros((blk, 1), F32), "diagonal")
        acc_p, after_p = block(q, c - 1, after_d, "previous")
        acc_ref[j] = acc_d + acc_p
        after_ref[j] = after_p

    for j in range(nsub):
        q = scaled_q(j)

        def cond(c, j=j):
            return jnp.logical_and(c >= 0, jnp.max(after_ref[j]) > SB_LOG_UNDERFLOW)

        def body(c, j=j, q=q):
            acc_c, after_c = block(q, c, after_ref[j], "tail")
            acc_ref[j] += acc_c
            after_ref[j] = after_c
            return c - 1

        lax.while_loop(cond, body, i * nsub + j - 2)
        o_ref[j * blk:(j + 1) * blk, :] = acc_ref[j].astype(o_ref.dtype)


def stick_breaking_attention(proj, *, batch, seq, blk, nsub):
    t = proj.shape[0]
    tq = blk * nsub
    nq = seq // tq
    dh = SB_HEAD_DIM
    return pl.pallas_call(
        functools.partial(_sb_kernel, blk=blk, nsub=nsub, scale=dh ** -0.5),
        grid=(batch, SB_HEADS, nq),
        in_specs=[
            pl.BlockSpec((tq, dh), lambda b, h, i: (b * nq + i, h)),
            pl.BlockSpec((seq, dh), lambda b, h, i: (b, SB_HEADS + h)),
            pl.BlockSpec((seq, dh), lambda b, h, i: (b, 2 * SB_HEADS + h)),
        ],
        out_specs=pl.BlockSpec((tq, dh), lambda b, h, i: (b * nq + i, h)),
        out_shape=jax.ShapeDtypeStruct((t, SB_WIDTH), BF16),
        scratch_shapes=[pltpu.VMEM((nsub, blk, dh), F32), pltpu.VMEM((nsub, blk, 1), F32)],
        compiler_params=_params("parallel", "parallel", "arbitrary"),
        name="stick_breaking",
    )(proj, proj, proj)


def _pool_kernel(u_ref, halo_ref, pw_ref, ps_ref, o_ref, *, ts):
    i = pl.program_id(1)
    halo = halo_ref[...].astype(F32)
    halo = jnp.where(i > 0, halo, 0.0)
    pos = i * ts + lax.broadcasted_iota(jnp.int32, (ts, 1), 0)
    c = POOL_GROUP_DIM
    for gi, win in enumerate(POOL_WINDOWS):
        u = u_ref[:, gi * c:(gi + 1) * c].astype(F32)
        ext = jnp.concatenate([halo[:, gi * c:(gi + 1) * c], u], axis=0)
        s, span = ext, 1
        while span < win:
            s = s[span:] + s[:-span]
            span *= 2
        wsum = s[POOL_HALO - win + 1:]
        count = jnp.minimum(pos + 1, win).astype(F32)
        pooled = wsum / count - u
        y = _dot(pooled.astype(BF16), pw_ref[gi])
        o_ref[:, gi * c:(gi + 1) * c] = (y * ps_ref[:, gi * c:(gi + 1) * c]).astype(o_ref.dtype)


def multiscale_pool(proj, pool_w, pool_scale, *, batch, seq, ts):
    t = proj.shape[0]
    ns = seq // ts
    width = pool_scale.shape[0]
    ublk = (3 * SB_WIDTH) // width
    hb = ts // POOL_HALO
    return pl.pallas_call(
        functools.partial(_pool_kernel, ts=ts),
        grid=(batch, ns),
        in_specs=[
            pl.BlockSpec((ts, width), lambda b, i: (b * ns + i, ublk)),
            pl.BlockSpec((POOL_HALO, width),
                         lambda b, i: (jnp.maximum((b * ns + i) * hb - 1, 0), ublk)),
            pl.BlockSpec(pool_w.shape, lambda b, i: (0, 0, 0)),
            pl.BlockSpec((1, width), lambda b, i: (0, 0)),
        ],
        out_specs=pl.BlockSpec((ts, width), lambda b, i: (b * ns + i, 0)),
        out_shape=jax.ShapeDtypeStruct((t, width), BF16),
        compiler_params=_params("parallel", "parallel"),
        name="multiscale_pool",
    )(proj, proj, pool_w, pool_scale.reshape(1, width))


def _gla_kernel(q_ref, k_ref, v_ref, a_ref, gw_ref, gb_ref, o_ref, state_ref,
                *, ts, chunk, nsc, dk, dv, scale):
    @pl.when(pl.program_id(2) == 0)
    def _():
        state_ref[...] = jnp.zeros_like(state_ref)

    heads = state_ref.shape[0]
    group = chunk * nsc
    gate_in = _dot(a_ref[...], gw_ref[...]) + gb_ref[...]
    g = (jnp.minimum(gate_in, 0.0) - _log1p_exp_neg_abs(gate_in)) * (1.0 / GLA_GATE_NORMALIZER)
    li = lax.broadcasted_iota(jnp.int32, (chunk, chunk), 0)
    mi = lax.broadcasted_iota(jnp.int32, (chunk, chunk), 1)
    causal = mi <= li
    causal_bf = causal.astype(BF16)
    causal2 = jnp.concatenate([causal_bf, causal_bf], axis=1)
    for c in range(ts // group):
        for hh in range(heads):
            kcols = slice(hh * dk, (hh + 1) * dk)
            vcols = slice(hh * dv, (hh + 1) * dv)
            state = state_ref[hh]
            q_loc, k_loc, k_loc_bf, b_end, v_sub, o_sub = [], [], [], [], [], []
            for si in range(nsc):
                sl = slice(c * group + si * chunk, c * group + (si + 1) * chunk)
                g_hi, g_lo = _split_bf16(g[sl, kcols])
                b = _dot(causal2, jnp.concatenate([g_hi, g_lo], axis=0))
                b_mid = b[chunk // 2:chunk // 2 + 1]
                b_last = b[chunk - 1:chunk]
                qc = q_ref[sl, kcols].astype(F32) * scale
                kc = k_ref[sl, kcols].astype(F32)
                vc = v_ref[sl, vcols]
                qi = (qc * jnp.exp(b - b_mid)).astype(BF16)
                ki = (kc * jnp.exp(b_mid - b)).astype(BF16)
                scores = jnp.where(causal, _dot_nt(qi, ki), 0.0).astype(BF16)
                o_sub.append(_dot(scores, vc))
                q_loc.append(qc * jnp.exp(b))
                k_loc.append(kc * jnp.exp(b_last - b))
                k_loc_bf.append(k_loc[si].astype(BF16))
                b_end.append(b_last)
                v_sub.append(vc)
            prefix = [None]
            for si in range(nsc):
                prefix.append(b_end[si] if prefix[si] is None else prefix[si] + b_end[si])
            total = prefix[nsc]
            q_inter, k_state = [], []
            for si in range(nsc):
                q_inter.append((q_loc[si] if si == 0 else q_loc[si] * jnp.exp(prefix[si])).astype(BF16))
                k_state.append(k_loc_bf[si] if si == nsc - 1
                               else (k_loc[si] * jnp.exp(total - prefix[si + 1])).astype(BF16))
                for sj in range(si):
                    qs = (q_loc[si] if sj == si - 1
                          else q_loc[si] * jnp.exp(prefix[si] - prefix[sj + 1])).astype(BF16)
                    cross = _dot_nt(qs, k_loc_bf[sj]).astype(BF16)
                    o_sub[si] = o_sub[si] + _dot(cross, v_sub[sj])
            inter = _dot(jnp.concatenate(q_inter, axis=0), state.astype(BF16))
            decay = jnp.broadcast_to(jnp.exp(total), (LANES, dk)).T[:, :1]
            state_ref[hh] = decay * state + _dot_tn(jnp.concatenate(k_state, axis=0),
                                                     jnp.concatenate(v_sub, axis=0))
            for si in range(nsc):
                sl = slice(c * group + si * chunk, c * group + (si + 1) * chunk)
                o_ref[sl, vcols] = (o_sub[si] + inter[si * chunk:(si + 1) * chunk]).astype(o_ref.dtype)


def gated_linear_attention(proj, a, gate_w, gate_b, *, batch, seq, ts):
    t = proj.shape[0]
    ns = seq // ts
    dk, dv, nh, hps = GLA_HK, GLA_HV, GLA_HEADS, GLA_HEADS_PER_STEP
    kw, vw = nh * dk, nh * dv
    ng = nh // hps
    bk, bv = hps * dk, hps * dv
    return pl.pallas_call(
        functools.partial(_gla_kernel, ts=ts, chunk=GLA_CHUNK, nsc=GLA_SUBCHUNKS, dk=dk, dv=dv,
                          scale=dk ** -0.5),
        grid=(batch, ng, ns),
        in_specs=[
            pl.BlockSpec((ts, bk), lambda b, h, s: (b * ns + s, h)),
            pl.BlockSpec((ts, bk), lambda b, h, s: (b * ns + s, ng + h)),
            pl.BlockSpec((ts, bv), lambda b, h, s: (b * ns + s, (2 * kw) // bv + h)),
            pl.BlockSpec((ts, LANES), lambda b, h, s: (b * ns + s, 0)),
            pl.BlockSpec((LANES, bk), lambda b, h, s: (0, h)),
            pl.BlockSpec((1, bk), lambda b, h, s: (0, h)),
        ],
        out_specs=pl.BlockSpec((ts, bv), lambda b, h, s: (b * ns + s, h)),
        out_shape=jax.ShapeDtypeStruct((t, vw), BF16),
        scratch_shapes=[pltpu.VMEM((hps, dk, dv), F32)],
        compiler_params=_params("parallel", "parallel", "arbitrary"),
        name="gated_linear_attention",
    )(proj, proj, proj, a, gate_w, gate_b.reshape(1, kw))


def _xattn_sublayer_kernel(x_ref, gpre_ref, wq_ref, k_ref, v_ref, wo_ref, gpost_ref, o_ref,
                           *, heads, scale, sub):
    dh = wq_ref.shape[1] // heads
    for r in range(x_ref.shape[0] // sub):
        rows = slice(r * sub, (r + 1) * sub)
        xr = x_ref[rows, :]
        h = _rms(xr, gpre_ref[...]).astype(BF16)
        q = _dot(h, wq_ref[...]).astype(BF16)
        y = None
        for hd in range(heads):
            sl = slice(hd * dh, (hd + 1) * dh)
            s = _dot_nt(q[:, sl], k_ref[:, sl]) * scale
            p = jnp.exp(s - jnp.max(s, axis=-1, keepdims=True))
            den = jnp.sum(p, axis=-1, keepdims=True)
            o_h = (_dot(p.astype(BF16), v_ref[:, sl]) / den).astype(BF16)
            part = _dot(o_h, wo_ref[sl, :])
            y = part if y is None else y + part
        o_ref[rows, :] = xr + _rms(y, gpost_ref[...])


def cross_attention_sublayer(x, g_pre, wq, kv, layer, wo, g_post, *, batch, seq, n_mem, tm):
    t, d = x.shape
    ns = seq // tm
    return pl.pallas_call(
        functools.partial(_xattn_sublayer_kernel, heads=X_HEADS, scale=(d // X_HEADS) ** -0.5,
                          sub=min(SUB_ROWS, tm)),
        grid=(batch, ns),
        in_specs=[
            pl.BlockSpec((tm, d), lambda b, s: (b * ns + s, 0)),
            _resident((1, d)),
            _resident_layer(wq.shape, layer),
            pl.BlockSpec((n_mem, d), lambda b, s: (b, 2 * layer)),
            pl.BlockSpec((n_mem, d), lambda b, s: (b, 2 * layer + 1)),
            _resident_layer(wo.shape, layer),
            _resident((1, d)),
        ],
        out_specs=pl.BlockSpec((tm, d), lambda b, s: (b * ns + s, 0)),
        out_shape=jax.ShapeDtypeStruct((t, d), F32),
        compiler_params=_params("parallel", "parallel"),
        name="cross_attention_sublayer",
    )(x, g_pre.reshape(1, d), wq, kv, kv, wo, g_post.reshape(1, d))


def _ffn_kernel(x_ref, gpre_ref, wgu_ref, wd_ref, gpost_ref, o_ref, h_ref, *, sub):
    f = pl.program_id(1)
    last = pl.num_programs(1) - 1
    tf = wd_ref.shape[0]

    def step(first, final):
        for r in range(x_ref.shape[0] // sub):
            rows = slice(r * sub, (r + 1) * sub)
            if first:
                h = _rms(x_ref[rows, :], gpre_ref[...]).astype(BF16)
                h_ref[rows, :] = h
            else:
                h = h_ref[rows, :]
            gate_up = _dot(h, wgu_ref[...])
            act = (_silu(gate_up[:, :tf]) * gate_up[:, tf:]).astype(BF16)
            y = _dot(act, wd_ref[...])
            if not first:
                y = o_ref[rows, :] + y
            if final:
                o_ref[rows, :] = x_ref[rows, :] + _rms(y, gpost_ref[...])
            else:
                o_ref[rows, :] = y

    pl.when(f == 0)(lambda: step(True, False))
    pl.when(jnp.logical_and(f > 0, f < last))(lambda: step(False, False))
    pl.when(f == last)(lambda: step(False, True))


def pack_gate_up(w_gate, w_up, tf):
    nl, d, dff = w_gate.shape
    tiles = lambda w: w.astype(BF16).reshape(nl, d, dff // tf, tf)
    return jnp.concatenate([tiles(w_gate), tiles(w_up)], axis=3).transpose(0, 2, 1, 3)


def swiglu_sublayer(x, g_pre, w_gate_up, w_down, layer, g_post, *, tm, sub):
    m, d = x.shape
    _, nf, _, tf2 = w_gate_up.shape
    tf = tf2 // 2
    assert nf >= 2
    return pl.pallas_call(
        functools.partial(_ffn_kernel, sub=sub),
        grid=(m // tm, nf),
        in_specs=[
            pl.BlockSpec((tm, d), lambda i, f: (i, 0)),
            _resident((1, d)),
            pl.BlockSpec((None, None, d, tf2), lambda i, f: (layer, f, 0, 0)),
            pl.BlockSpec((None, tf, d), lambda i, f: (layer, f, 0)),
            _resident((1, d)),
        ],
        out_specs=pl.BlockSpec((tm, d), lambda i, f: (i, 0)),
        out_shape=jax.ShapeDtypeStruct((m, d), F32),
        scratch_shapes=[pltpu.VMEM((tm, d), BF16)],
        compiler_params=_params("parallel", "arbitrary"),
        name="swiglu_sublayer",
    )(x, g_pre.reshape(1, d), w_gate_up, w_down, g_post.reshape(1, d))


def kernel(x, mem, norms, mem_norm, even_w_in, pool_w, pool_scale, even_w_out,
           odd_w_in, gla_gate_w, gla_gate_b, gla_gnorm, odd_w_out,
           xattn_wq, xattn_wk, xattn_wv, xattn_wo,
           ffn_w_gate, ffn_w_up, ffn_w_down):
    batch, seq, d = x.shape
    n_mem = mem.shape[1]
    depth = norms.shape[0]
    t = batch * seq
    bf = lambda w: w.astype(BF16)

    wq, wo = bf(xattn_wq), bf(xattn_wo)
    w_gate_up = pack_gate_up(ffn_w_gate, ffn_w_up, FFN_HIDDEN_TILE)
    w_down = bf(ffn_w_down)
    w_kv = jnp.concatenate([bf(w) for l in range(depth) for w in (xattn_wk[l], xattn_wv[l])], axis=1)

    xs = x.reshape(t, d)
    kv = norm_matmul(mem.reshape(batch * n_mem, d), mem_norm, w_kv, tm=SUB_ROWS)

    for layer in range(depth):
        g = norms[layer]
        i = layer // 2
        if layer % 2 == 0:
            proj = norm_matmul(xs, g[0], bf(even_w_in[i]), tm=1024)
            o_sb = stick_breaking_attention(proj, batch=batch, seq=seq, blk=256, nsub=4)
            o_pool = multiscale_pool(proj, bf(pool_w[i]), pool_scale[i], batch=batch, seq=seq, ts=512)
            xs = matmul_norm_res([o_sb, o_pool], bf(even_w_out[i]), g[1], xs, tm=1024)
        else:
            n_main = 2 * GLA_HEADS * GLA_HK + 2 * GLA_HEADS * GLA_HV
            w_a = jnp.pad(bf(odd_w_in[i][:, n_main:]), ((0, 0), (0, LANES - GLA_GATE_RANK)))
            gate_w = jnp.pad(bf(gla_gate_w[i]), ((0, LANES - GLA_GATE_RANK), (0, 0)))
            proj, a = norm_matmul(xs, g[0], bf(odd_w_in[i][:, :n_main]), w_a, tm=512)
            o = gated_linear_attention(proj, a, gate_w, gla_gate_b[i], batch=batch, seq=seq, ts=512)
            xs = gated_matmul_norm_res(o, proj, n_main // o.shape[1] - 1, gla_gnorm[i],
                                       bf(odd_w_out[i]), g[1], xs, heads=GLA_HEADS, tm=512)
        xs = cross_attention_sublayer(xs, g[2], wq, kv, layer, wo, g[3],
                                      batch=batch, seq=seq, n_mem=n_mem, tm=512)
        xs = swiglu_sublayer(xs, g[4], w_gate_up, w_down, layer, g[5], tm=1024, sub=512)
    return xs.reshape(batch, seq, d)
```

```python
import functools

import jax
import jax.numpy as jnp
from jax import lax
from jax.experimental import pallas as pl
from jax.experimental.pallas import tpu as pltpu

F32 = jnp.float32
BF16 = jnp.bfloat16

EPS = 1e-6

SB_HEADS = 4
SB_HEAD_DIM = 256
SB_WIDTH = SB_HEADS * SB_HEAD_DIM
SB_MASK_VALUE = -1e4
SB_LOG_UNDERFLOW = -88.0
POOL_WINDOWS = (2, 4, 8, 16)
POOL_HALO = 16
POOL_GROUP_DIM = 256

GLA_HEADS = 4
GLA_HK = 256
GLA_HV = 512
GLA_GATE_RANK = 16
GLA_GATE_NORMALIZER = 16.0
GLA_CHUNK = 64
GLA_SUBCHUNKS = 2
GLA_HEADS_PER_STEP = 2

X_HEADS = 4
FFN_HIDDEN_TILE = 512

V7X_VMEM_LIMIT_BYTES = 60 * 1024 * 1024
LANES = 128
SUB_ROWS = 256


def _params(*sem):
    return pltpu.CompilerParams(dimension_semantics=sem, vmem_limit_bytes=V7X_VMEM_LIMIT_BYTES)


def _resident(shape):
    zeros = (0,) * len(shape)
    return pl.BlockSpec(shape, lambda *_: zeros, pipeline_mode=pl.Buffered(1))


def _resident_layer(shape, layer):
    index = (layer,) + (0,) * (len(shape) - 1)
    return pl.BlockSpec((None,) + tuple(shape[1:]), lambda *_: index, pipeline_mode=pl.Buffered(1))


def _rms(x, g):
    ms = jnp.mean(x * x, axis=-1, keepdims=True)
    return x * lax.rsqrt(ms + EPS) * g


def _dot(a, b):
    return jnp.dot(a, b, preferred_element_type=F32)


def _dot_nt(a, b):
    return lax.dot_general(a, b, (((1,), (1,)), ((), ())), preferred_element_type=F32)


def _dot_tn(a, b):
    return lax.dot_general(a, b, (((0,), (0,)), ((), ())), preferred_element_type=F32)


def _split_bf16(x):
    hi = x.astype(BF16)
    lo = (x - hi.astype(F32)).astype(BF16)
    return hi, lo


def _log1p_exp_neg_abs(x):
    return jnp.log(1.0 + jnp.exp(-jnp.abs(x)))


def _silu(x):
    return x * jax.nn.sigmoid(x)


def _cast_kernel(w_ref, o_ref):
    o_ref[...] = w_ref[...].astype(o_ref.dtype)


def cast_bf16(w, *, tr, cols=None):
    nl, r, c = w.shape
    cols = c if cols is None else cols
    spec = pl.BlockSpec((None, tr, cols), lambda l, i: (l, i, 0))
    return pl.pallas_call(
        _cast_kernel,
        grid=(nl, r // tr),
        in_specs=[spec],
        out_specs=spec,
        out_shape=jax.ShapeDtypeStruct((nl, r, cols), BF16),
        compiler_params=_params("parallel", "parallel"),
        name="cast_bf16",
    )(w)


def _cast_split_kernel(w_ref, main_ref, tail_ref):
    cols = main_ref.shape[1]
    tail = w_ref.shape[1] - cols
    main_ref[...] = w_ref[:, :cols].astype(main_ref.dtype)
    tail_ref[...] = jnp.zeros_like(tail_ref)
    tail_ref[:, :tail] = w_ref[:, cols:].astype(tail_ref.dtype)


def cast_bf16_split(w, *, tr, cols):
    nl, r, c = w.shape
    assert 0 < c - cols <= LANES
    return pl.pallas_call(
        _cast_split_kernel,
        grid=(nl, r // tr),
        in_specs=[pl.BlockSpec((None, tr, c), lambda l, i: (l, i, 0))],
        out_specs=[pl.BlockSpec((None, tr, cols), lambda l, i: (l, i, 0)),
                   pl.BlockSpec((None, tr, LANES), lambda l, i: (l, i, 0))],
        out_shape=[jax.ShapeDtypeStruct((nl, r, cols), BF16), jax.ShapeDtypeStruct((nl, r, LANES), BF16)],
        compiler_params=_params("parallel", "parallel"),
        name="cast_bf16_split",
    )(w)


def _pack_gate_up_kernel(g_ref, u_ref, o_ref):
    nf, _, tf2 = o_ref.shape
    tf = tf2 // 2
    for f in range(nf):
        o_ref[f, :, :tf] = g_ref[:, f * tf:(f + 1) * tf].astype(o_ref.dtype)
        o_ref[f, :, tf:] = u_ref[:, f * tf:(f + 1) * tf].astype(o_ref.dtype)


def pack_gate_up(w_gate, w_up, tf, *, tr):
    nl, d, dff = w_gate.shape
    nf = dff // tf
    spec = pl.BlockSpec((None, tr, dff), lambda l, i: (l, i, 0))
    return pl.pallas_call(
        _pack_gate_up_kernel,
        grid=(nl, d // tr),
        in_specs=[spec, spec],
        out_specs=pl.BlockSpec((None, nf, tr, 2 * tf), lambda l, i: (l, 0, i, 0)),
        out_shape=jax.ShapeDtypeStruct((nl, nf, d, 2 * tf), BF16),
        compiler_params=_params("parallel", "parallel"),
        name="pack_gate_up",
    )(w_gate, w_up)


def _norm_matmul_kernel(x_ref, g_ref, w_ref, *rest, has_extra, sub):
    if has_extra:
        w2_ref, o_ref, o2_ref = rest
    else:
        (o_ref,) = rest
    for r in range(x_ref.shape[0] // sub):
        rows = slice(r * sub, (r + 1) * sub)
        h = _rms(x_ref[rows, :], g_ref[...]).astype(BF16)
        o_ref[rows, :] = _dot(h, w_ref[...]).astype(o_ref.dtype)
        if has_extra:
            o2_ref[rows, :] = _dot(h, w2_ref[...]).astype(o2_ref.dtype)


def norm_matmul(x, g, w, w_extra=None, *, tm, layer=None):
    m, k = x.shape
    n = w.shape[-1]
    has_extra = w_extra is not None
    w_spec = _resident((k, n)) if layer is None else _resident_layer(w.shape, layer)
    in_specs = [pl.BlockSpec((tm, k), lambda i: (i, 0)), _resident((1, k)), w_spec]
    out_specs = [pl.BlockSpec((tm, n), lambda i: (i, 0))]
    out_shape = [jax.ShapeDtypeStruct((m, n), BF16)]
    args = [x, g.reshape(1, k), w]
    if has_extra:
        n2 = w_extra.shape[1]
        in_specs.append(_resident((k, n2)))
        out_specs.append(pl.BlockSpec((tm, n2), lambda i: (i, 0)))
        out_shape.append(jax.ShapeDtypeStruct((m, n2), BF16))
        args.append(w_extra)
    out = pl.pallas_call(
        functools.partial(_norm_matmul_kernel, has_extra=has_extra, sub=min(SUB_ROWS, tm)),
        grid=(m // tm,),
        in_specs=in_specs,
        out_specs=out_specs,
        out_shape=out_shape,
        compiler_params=_params("parallel"),
        name="norm_matmul",
    )(*args)
    return out if has_extra else out[0]


def _matmul_norm_res_kernel(*refs, n_in, sub):
    a_refs = refs[:n_in]
    w_ref, g_ref, x_ref, o_ref = refs[n_in:]
    for r in range(x_ref.shape[0] // sub):
        rows = slice(r * sub, (r + 1) * sub)
        y, k0 = None, 0
        for a_ref in a_refs:
            k1 = k0 + a_ref.shape[1]
            part = _dot(a_ref[rows, :], w_ref[k0:k1, :])
            y = part if y is None else y + part
            k0 = k1
        o_ref[rows, :] = x_ref[rows, :] + _rms(y, g_ref[...])


def _gated_matmul_norm_res_kernel(o_ref, r_ref, gn_ref, w_ref, g_ref, x_ref, out_ref, *, heads, sub):
    dv = o_ref.shape[1] // heads
    for r in range(x_ref.shape[0] // sub):
        rows = slice(r * sub, (r + 1) * sub)
        y = None
        for hd in range(heads):
            sl = slice(hd * dv, (hd + 1) * dv)
            a = _rms(o_ref[rows, sl].astype(F32), gn_ref[...]) * _silu(r_ref[rows, sl].astype(F32))
            part = _dot(a.astype(BF16), w_ref[sl, :])
            y = part if y is None else y + part
        out_ref[rows, :] = x_ref[rows, :] + _rms(y, g_ref[...])


def gated_matmul_norm_res(o, proj, r_block, gnorm, w, g, x, *, heads, tm, sub):
    m, n = x.shape
    kdim = o.shape[1]
    return pl.pallas_call(
        functools.partial(_gated_matmul_norm_res_kernel, heads=heads, sub=sub),
        grid=(m // tm,),
        in_specs=[
            pl.BlockSpec((tm, kdim), lambda i: (i, 0)),
            pl.BlockSpec((tm, kdim), lambda i: (i, r_block)),
            _resident((1, kdim // heads)),
            _resident(w.shape),
            _resident((1, n)),
            pl.BlockSpec((tm, n), lambda i: (i, 0)),
        ],
        out_specs=pl.BlockSpec((tm, n), lambda i: (i, 0)),
        out_shape=jax.ShapeDtypeStruct((m, n), F32),
        compiler_params=_params("parallel"),
        name="gated_matmul_norm_res",
    )(o, proj, gnorm.reshape(1, kdim // heads), w, g.reshape(1, n), x)


def matmul_norm_res(a_list, w, g, x, *, tm, sub):
    m, n = x.shape
    in_specs = [pl.BlockSpec((tm, a.shape[1]), lambda i: (i, 0)) for a in a_list]
    in_specs += [_resident(w.shape), _resident((1, n)), pl.BlockSpec((tm, n), lambda i: (i, 0))]
    return pl.pallas_call(
        functools.partial(_matmul_norm_res_kernel, n_in=len(a_list), sub=sub),
        grid=(m // tm,),
        in_specs=in_specs,
        out_specs=pl.BlockSpec((tm, n), lambda i: (i, 0)),
        out_shape=jax.ShapeDtypeStruct((m, n), F32),
        compiler_params=_params("parallel"),
        name="matmul_norm_res",
    )(*a_list, w, g.reshape(1, n), x)


def _sb_kernel(q_ref, k_ref, v_ref, o_ref, acc_ref, after_ref, *, blk, nsub, scale):
    i = pl.program_id(2)
    rows = lax.broadcasted_iota(jnp.int32, (blk, blk), 0)
    cols = lax.broadcasted_iota(jnp.int32, (blk, blk), 1)
    later = (rows > cols).astype(BF16)
    later2 = jnp.concatenate([later, later], axis=0)

    def scaled_q(j):
        return (q_ref[j * blk:(j + 1) * blk, :].astype(F32) * scale).astype(BF16)

    def block(q, c, after, kind):
        start = pl.multiple_of(jnp.maximum(c, 0) * blk, blk)
        kc = k_ref[pl.ds(start, blk), :]
        vc = v_ref[pl.ds(start, blk), :]
        z = _dot_nt(q, kc)
        if kind == "diagonal":
            z = jnp.where(cols < rows, z, SB_MASK_VALUE)
        elif kind == "previous":
            z = jnp.where(c >= 0, z, SB_MASK_VALUE)
        smooth = _log1p_exp_neg_abs(z)
        log_stay = jnp.minimum(-z, 0.0) - smooth
        log_beta = jnp.minimum(z, 0.0) - smooth
        hi, lo = _split_bf16(log_stay)
        within = _dot(jnp.concatenate([hi, lo], axis=1), later2)
        w = jnp.exp(log_beta + within + after)
        return _dot(w.astype(BF16), vc), after + jnp.sum(log_stay, axis=-1, keepdims=True)

    for j in range(nsub):
        c = i * nsub + j
        q = scaled_q(j)
        acc_d, after_d = block(q, c, jnp.zeros((blk, 1), F32), "diagonal")
        acc_p, after_p = block(q, c - 1, after_d, "previous")
        acc_ref[j] = acc_d + acc_p
        after_ref[j] = after_p

    for j in range(nsub):
        q = scaled_q(j)

        def cond(c, j=j):
            return jnp.logical_and(c >= 0, jnp.max(after_ref[j]) > SB_LOG_UNDERFLOW)

        def body(c, j=j, q=q):
            acc_c, after_c = block(q, c, after_ref[j], "tail")
            acc_ref[j] += acc_c
            after_ref[j] = after_c
            return c - 1

        lax.while_loop(cond, body, i * nsub + j - 2)
        o_ref[j * blk:(j + 1) * blk, :] = acc_ref[j].astype(o_ref.dtype)


def stick_breaking_attention(proj, *, batch, seq, blk, nsub):
    t = proj.shape[0]
    tq = blk * nsub
    nq = seq // tq
    dh = SB_HEAD_DIM
    return pl.pallas_call(
        functools.partial(_sb_kernel, blk=blk, nsub=nsub, scale=dh ** -0.5),
        grid=(batch, SB_HEADS, nq),
        in_specs=[
            pl.BlockSpec((tq, dh), lambda b, h, i: (b * nq + i, h)),
            pl.BlockSpec((seq, dh), lambda b, h, i: (b, SB_HEADS + h)),
            pl.BlockSpec((seq, dh), lambda b, h, i: (b, 2 * SB_HEADS + h)),
        ],
        out_specs=pl.BlockSpec((tq, dh), lambda b, h, i: (b * nq + i, h)),
        out_shape=jax.ShapeDtypeStruct((t, SB_WIDTH), BF16),
        scratch_shapes=[pltpu.VMEM((nsub, blk, dh), F32), pltpu.VMEM((nsub, blk, 1), F32)],
        compiler_params=_params("parallel", "parallel", "arbitrary"),
        name="stick_breaking",
    )(proj, proj, proj)


def _pool_kernel(u_ref, halo_ref, pw_ref, ps_ref, o_ref, *, ts):
    i = pl.program_id(1)
    halo = halo_ref[...].astype(F32)
    halo = jnp.where(i > 0, halo, 0.0)
    pos = i * ts + lax.broadcasted_iota(jnp.int32, (ts, 1), 0)
    c = POOL_GROUP_DIM
    for gi, win in enumerate(POOL_WINDOWS):
        u = u_ref[:, gi * c:(gi + 1) * c].astype(F32)
        ext = jnp.concatenate([halo[:, gi * c:(gi + 1) * c], u], axis=0)
        s, span = ext, 1
        while span < win:
            s = s[span:] + s[:-span]
            span *= 2
        wsum = s[POOL_HALO - win + 1:]
        count = jnp.minimum(pos + 1, win).astype(F32)
        pooled = wsum / count - u
        y = _dot(pooled.astype(BF16), pw_ref[gi])
        o_ref[:, gi * c:(gi + 1) * c] = (y * ps_ref[:, gi * c:(gi + 1) * c]).astype(o_ref.dtype)


def multiscale_pool(proj, pool_w, pool_scale, *, batch, seq, ts):
    t = proj.shape[0]
    ns = seq // ts
    width = pool_scale.shape[0]
    ublk = (3 * SB_WIDTH) // width
    hb = ts // POOL_HALO
    return pl.pallas_call(
        functools.partial(_pool_kernel, ts=ts),
        grid=(batch, ns),
        in_specs=[
            pl.BlockSpec((ts, width), lambda b, i: (b * ns + i, ublk)),
            pl.BlockSpec((POOL_HALO, width),
                         lambda b, i: (jnp.maximum((b * ns + i) * hb - 1, 0), ublk)),
            pl.BlockSpec(pool_w.shape, lambda b, i: (0, 0, 0)),
            pl.BlockSpec((1, width), lambda b, i: (0, 0)),
        ],
        out_specs=pl.BlockSpec((ts, width), lambda b, i: (b * ns + i, 0)),
        out_shape=jax.ShapeDtypeStruct((t, width), BF16),
        compiler_params=_params("parallel", "parallel"),
        name="multiscale_pool",
    )(proj, proj, pool_w, pool_scale.reshape(1, width))


def _gla_kernel(q_ref, k_ref, v_ref, a_ref, gw_ref, gb_ref, o_ref, state_ref,
                *, ts, chunk, nsc, dk, dv, scale):
    @pl.when(pl.program_id(2) == 0)
    def _():
        state_ref[...] = jnp.zeros_like(state_ref)

    heads = state_ref.shape[0]
    group = chunk * nsc
    gate_in = _dot(a_ref[...], gw_ref[...]) + gb_ref[...]
    g = (jnp.minimum(gate_in, 0.0) - _log1p_exp_neg_abs(gate_in)) * (1.0 / GLA_GATE_NORMALIZER)
    li = lax.broadcasted_iota(jnp.int32, (chunk, chunk), 0)
    mi = lax.broadcasted_iota(jnp.int32, (chunk, chunk), 1)
    causal = mi <= li
    causal_bf = causal.astype(BF16)
    causal2 = jnp.concatenate([causal_bf, causal_bf], axis=1)
    for c in range(ts // group):
        for hh in range(heads):
            kcols = slice(hh * dk, (hh + 1) * dk)
            vcols = slice(hh * dv, (hh + 1) * dv)
            state = state_ref[hh]
            q_loc, k_loc, k_loc_bf, b_end, v_sub, o_sub = [], [], [], [], [], []
            for si in range(nsc):
                sl = slice(c * group + si * chunk, c * group + (si + 1) * chunk)
                g_hi, g_lo = _split_bf16(g[sl, kcols])
                b = _dot(causal2, jnp.concatenate([g_hi, g_lo], axis=0))
                b_mid = b[chunk // 2:chunk // 2 + 1]
                b_last = b[chunk - 1:chunk]
                qc = q_ref[sl, kcols].astype(F32) * scale
                kc = k_ref[sl, kcols].astype(F32)
                vc = v_ref[sl, vcols]
                qi = (qc * jnp.exp(b - b_mid)).astype(BF16)
                ki = (kc * jnp.exp(b_mid - b)).astype(BF16)
                scores = jnp.where(causal, _dot_nt(qi, ki), 0.0).astype(BF16)
                o_sub.append(_dot(scores, vc))
                q_loc.append(qc * jnp.exp(b))
                k_loc.append(kc * jnp.exp(b_last - b))
                k_loc_bf.append(k_loc[si].astype(BF16))
                b_end.append(b_last)
                v_sub.append(vc)
            prefix = [None]
            for si in range(nsc):
                prefix.append(b_end[si] if prefix[si] is None else prefix[si] + b_end[si])
            total = prefix[nsc]
            q_inter, k_state = [], []
            for si in range(nsc):
                q_inter.append((q_loc[si] if si == 0 else q_loc[si] * jnp.exp(prefix[si])).astype(BF16))
                k_state.append(k_loc_bf[si] if si == nsc - 1
                               else (k_loc[si] * jnp.exp(total - prefix[si + 1])).astype(BF16))
                for sj in range(si):
                    qs = (q_loc[si] if sj == si - 1
                          else q_loc[si] * jnp.exp(prefix[si] - prefix[sj + 1])).astype(BF16)
                    cross = _dot_nt(qs, k_loc_bf[sj]).astype(BF16)
                    o_sub[si] = o_sub[si] + _dot(cross, v_sub[sj])
            inter = _dot(jnp.concatenate(q_inter, axis=0), state.astype(BF16))
            decay = jnp.broadcast_to(jnp.exp(total), (LANES, dk)).T[:, :1]
            state_ref[hh] = decay * state + _dot_tn(jnp.concatenate(k_state, axis=0),
                                                     jnp.concatenate(v_sub, axis=0))
            for si in range(nsc):
                sl = slice(c * group + si * chunk, c * group + (si + 1) * chunk)
                o_ref[sl, vcols] = (o_sub[si] + inter[si * chunk:(si + 1) * chunk]).astype(o_ref.dtype)


def gated_linear_attention(proj, a, gate_w, gate_b, *, batch, seq, ts):
    t = proj.shape[0]
    ns = seq // ts
    dk, dv, nh, hps = GLA_HK, GLA_HV, GLA_HEADS, GLA_HEADS_PER_STEP
    kw, vw = nh * dk, nh * dv
    ng = nh // hps
    bk, bv = hps * dk, hps * dv
    return pl.pallas_call(
        functools.partial(_gla_kernel, ts=ts, chunk=GLA_CHUNK, nsc=GLA_SUBCHUNKS, dk=dk, dv=dv,
                          scale=dk ** -0.5),
        grid=(batch, ng, ns),
        in_specs=[
            pl.BlockSpec((ts, bk), lambda b, h, s: (b * ns + s, h)),
            pl.BlockSpec((ts, bk), lambda b, h, s: (b * ns + s, ng + h)),
            pl.BlockSpec((ts, bv), lambda b, h, s: (b * ns + s, (2 * kw) // bv + h)),
            pl.BlockSpec((ts, LANES), lambda b, h, s: (b * ns + s, 0)),
            pl.BlockSpec((LANES, bk), lambda b, h, s: (0, h)),
            pl.BlockSpec((1, bk), lambda b, h, s: (0, h)),
        ],
        out_specs=pl.BlockSpec((ts, bv), lambda b, h, s: (b * ns + s, h)),
        out_shape=jax.ShapeDtypeStruct((t, vw), BF16),
        scratch_shapes=[pltpu.VMEM((hps, dk, dv), F32)],
        compiler_params=_params("parallel", "parallel", "arbitrary"),
        name="gated_linear_attention",
    )(proj, proj, proj, a, gate_w, gate_b.reshape(1, kw))


def _xattn_sublayer_kernel(x_ref, gpre_ref, wq_ref, k_ref, v_ref, wo_ref, gpost_ref, o_ref,
                           *, heads, scale, sub):
    dh = wq_ref.shape[1] // heads
    for r in range(x_ref.shape[0] // sub):
        rows = slice(r * sub, (r + 1) * sub)
        xr = x_ref[rows, :]
        h = _rms(xr, gpre_ref[...]).astype(BF16)
        q = _dot(h, wq_ref[...]).astype(BF16)
        y = None
        for hd in range(heads):
            sl = slice(hd * dh, (hd + 1) * dh)
            s = _dot_nt(q[:, sl], k_ref[:, sl]) * scale
            p = jnp.exp(s - jnp.max(s, axis=-1, keepdims=True))
            den = jnp.sum(p, axis=-1, keepdims=True)
            o_h = (_dot(p.astype(BF16), v_ref[:, sl]) / den).astype(BF16)
            part = _dot(o_h, wo_ref[sl, :])
            y = part if y is None else y + part
        o_ref[rows, :] = xr + _rms(y, gpost_ref[...])


def cross_attention_sublayer(x, g_pre, wq, k, v, layer, wo, g_post, *, batch, seq, n_mem, tm, sub):
    t, d = x.shape
    ns = seq // tm
    return pl.pallas_call(
        functools.partial(_xattn_sublayer_kernel, heads=X_HEADS, scale=(d // X_HEADS) ** -0.5,
                          sub=sub),
        grid=(batch, ns),
        in_specs=[
            pl.BlockSpec((tm, d), lambda b, s: (b * ns + s, 0)),
            _resident((1, d)),
            _resident_layer(wq.shape, layer),
            pl.BlockSpec((n_mem, d), lambda b, s: (b, 0), pipeline_mode=pl.Buffered(1)),
            pl.BlockSpec((n_mem, d), lambda b, s: (b, 0), pipeline_mode=pl.Buffered(1)),
            _resident_layer(wo.shape, layer),
            _resident((1, d)),
        ],
        out_specs=pl.BlockSpec((tm, d), lambda b, s: (b * ns + s, 0)),
        out_shape=jax.ShapeDtypeStruct((t, d), F32),
        compiler_params=_params("parallel", "parallel"),
        name="cross_attention_sublayer",
    )(x, g_pre.reshape(1, d), wq, k, v, wo, g_post.reshape(1, d))


def _ffn_kernel(x_ref, gpre_ref, wgu_ref, wd_ref, gpost_ref, o_ref, h_ref, *, sub):
    f = pl.program_id(1)
    last = pl.num_programs(1) - 1
    tf = wd_ref.shape[0]

    def step(first, final):
        for r in range(x_ref.shape[0] // sub):
            rows = slice(r * sub, (r + 1) * sub)
            if first:
                h = _rms(x_ref[rows, :], gpre_ref[...]).astype(BF16)
                h_ref[rows, :] = h
            else:
                h = h_ref[rows, :]
            gate_up = _dot(h, wgu_ref[...])
            act = (_silu(gate_up[:, :tf]) * gate_up[:, tf:]).astype(BF16)
            y = _dot(act, wd_ref[...])
            if not first:
                y = o_ref[rows, :] + y
            if final:
                o_ref[rows, :] = x_ref[rows, :] + _rms(y, gpost_ref[...])
            else:
                o_ref[rows, :] = y

    pl.when(f == 0)(lambda: step(True, False))
    pl.when(jnp.logical_and(f > 0, f < last))(lambda: step(False, False))
    pl.when(f == last)(lambda: step(False, True))


def swiglu_sublayer(x, g_pre, w_gate_up, w_down, layer, g_post, *, tm, sub):
    m, d = x.shape
    _, nf, _, tf2 = w_gate_up.shape
    tf = tf2 // 2
    assert nf >= 2
    return pl.pallas_call(
        functools.partial(_ffn_kernel, sub=sub),
        grid=(m // tm, nf),
        in_specs=[
            pl.BlockSpec((tm, d), lambda i, f: (i, 0)),
            _resident((1, d)),
            pl.BlockSpec((None, None, d, tf2), lambda i, f: (layer, f, 0, 0)),
            pl.BlockSpec((None, tf, d), lambda i, f: (layer, f, 0)),
            _resident((1, d)),
        ],
        out_specs=pl.BlockSpec((tm, d), lambda i, f: (i, 0)),
        out_shape=jax.ShapeDtypeStruct((m, d), F32),
        scratch_shapes=[pltpu.VMEM((tm, d), BF16)],
        compiler_params=_params("parallel", "arbitrary"),
        name="swiglu_sublayer",
    )(x, g_pre.reshape(1, d), w_gate_up, w_down, g_post.reshape(1, d))


def kernel(x, mem, norms, mem_norm, even_w_in, pool_w, pool_scale, even_w_out,
           odd_w_in, gla_gate_w, gla_gate_b, gla_gnorm, odd_w_out,
           xattn_wq, xattn_wk, xattn_wv, xattn_wo,
           ffn_w_gate, ffn_w_up, ffn_w_down):
    batch, seq, d = x.shape
    n_mem = mem.shape[1]
    depth = norms.shape[0]
    t = batch * seq
    bf = lambda w: w.astype(BF16)

    n_main = 2 * GLA_HEADS * GLA_HK + 2 * GLA_HEADS * GLA_HV
    wq, wo = cast_bf16(xattn_wq, tr=512), cast_bf16(xattn_wo, tr=512)
    w_gate_up = pack_gate_up(ffn_w_gate, ffn_w_up, FFN_HIDDEN_TILE, tr=256)
    w_down = cast_bf16(ffn_w_down, tr=512)
    w_in_even, w_out_even = cast_bf16(even_w_in, tr=256), cast_bf16(even_w_out, tr=512)
    (w_in_odd, w_a_odd), w_out_odd = cast_bf16_split(odd_w_in, tr=256, cols=n_main), cast_bf16(odd_w_out, tr=512)
    wk, wv = cast_bf16(xattn_wk, tr=512), cast_bf16(xattn_wv, tr=512)

    xs = x.reshape(t, d)
    mem2 = mem.reshape(batch * n_mem, d)
    mem_k = [norm_matmul(mem2, mem_norm, wk, tm=batch * n_mem, layer=l) for l in range(depth)]
    mem_v = [norm_matmul(mem2, mem_norm, wv, tm=batch * n_mem, layer=l) for l in range(depth)]

    for layer in range(depth):
        g = norms[layer]
        i = layer // 2
        if layer % 2 == 0:
            proj = norm_matmul(xs, g[0], w_in_even[i], tm=1024)
            o_sb = stick_breaking_attention(proj, batch=batch, seq=seq, blk=256, nsub=4)
            o_pool = multiscale_pool(proj, bf(pool_w[i]), pool_scale[i], batch=batch, seq=seq, ts=512)
            xs = matmul_norm_res([o_sb, o_pool], w_out_even[i], g[1], xs, tm=1024, sub=512)
        else:
            gate_w = jnp.pad(bf(gla_gate_w[i]), ((0, LANES - GLA_GATE_RANK), (0, 0)))
            proj, a = norm_matmul(xs, g[0], w_in_odd[i], w_a_odd[i], tm=512)
            o = gated_linear_attention(proj, a, gate_w, gla_gate_b[i], batch=batch, seq=seq, ts=512)
            xs = gated_matmul_norm_res(o, proj, n_main // o.shape[1] - 1, gla_gnorm[i],
                                       w_out_odd[i], g[1], xs, heads=GLA_HEADS, tm=512, sub=256)
        xs = cross_attention_sublayer(xs, g[2], wq, mem_k[layer], mem_v[layer], layer, wo, g[3],
                                      batch=batch, seq=seq, n_mem=n_mem, tm=512, sub=512)
        xs = swiglu_sublayer(xs, g[4], w_gate_up, w_down, layer, g[5], tm=1024, sub=512)
    return xs.reshape(batch, seq, d)
```

```python
import functools

import jax
import jax.numpy as jnp
from jax import lax
from jax.experimental import pallas as pl
from jax.experimental.pallas import tpu as pltpu

F32 = jnp.float32
BF16 = jnp.bfloat16

EPS = 1e-6

SB_HEADS = 4
SB_HEAD_DIM = 256
SB_WIDTH = SB_HEADS * SB_HEAD_DIM
SB_MASK_VALUE = -1e4
SB_LOG_UNDERFLOW = -88.0
POOL_WINDOWS = (2, 4, 8, 16)
POOL_HALO = 16
POOL_GROUP_DIM = 256

GLA_HEADS = 4
GLA_HK = 256
GLA_HV = 512
GLA_GATE_RANK = 16
GLA_GATE_NORMALIZER = 16.0
GLA_CHUNK = 64
GLA_SUBCHUNKS = 2
GLA_HEADS_PER_STEP = 2

X_HEADS = 4
FFN_HIDDEN_TILE = 512

V7X_VMEM_LIMIT_BYTES = 60 * 1024 * 1024
LANES = 128
SUB_ROWS = 256


def _params(*sem):
    return pltpu.CompilerParams(dimension_semantics=sem, vmem_limit_bytes=V7X_VMEM_LIMIT_BYTES)


def _resident(shape):
    zeros = (0,) * len(shape)
    return pl.BlockSpec(shape, lambda *_: zeros, pipeline_mode=pl.Buffered(1))


def _resident_layer(shape, layer):
    index = (layer,) + (0,) * (len(shape) - 1)
    return pl.BlockSpec((None,) + tuple(shape[1:]), lambda *_: index, pipeline_mode=pl.Buffered(1))


def _rms(x, g):
    ms = jnp.mean(x * x, axis=-1, keepdims=True)
    return x * lax.rsqrt(ms + EPS) * g


def _dot(a, b):
    return jnp.dot(a, b, preferred_element_type=F32)


def _dot_nt(a, b):
    return lax.dot_general(a, b, (((1,), (1,)), ((), ())), preferred_element_type=F32)


def _dot_tn(a, b):
    return lax.dot_general(a, b, (((0,), (0,)), ((), ())), preferred_element_type=F32)


def _split_bf16(x):
    hi = x.astype(BF16)
    lo = (x - hi.astype(F32)).astype(BF16)
    return hi, lo


def _log1p_exp_neg_abs(x):
    return jnp.log(1.0 + jnp.exp(-jnp.abs(x)))


def _silu(x):
    return x * jax.nn.sigmoid(x)


def _cast_kernel(w_ref, o_ref):
    o_ref[...] = w_ref[...].astype(o_ref.dtype)


def cast_bf16(w, *, tr, cols=None):
    nl, r, c = w.shape
    cols = c if cols is None else cols
    spec = pl.BlockSpec((None, tr, cols), lambda l, i: (l, i, 0))
    return pl.pallas_call(
        _cast_kernel,
        grid=(nl, r // tr),
        in_specs=[spec],
        out_specs=spec,
        out_shape=jax.ShapeDtypeStruct((nl, r, cols), BF16),
        compiler_params=_params("parallel", "parallel"),
        name="cast_bf16",
    )(w)


def _cast_split_kernel(w_ref, main_ref, tail_ref):
    cols = main_ref.shape[1]
    tail = w_ref.shape[1] - cols
    main_ref[...] = w_ref[:, :cols].astype(main_ref.dtype)
    tail_ref[...] = jnp.zeros_like(tail_ref)
    tail_ref[:, :tail] = w_ref[:, cols:].astype(tail_ref.dtype)


def cast_bf16_split(w, *, tr, cols):
    nl, r, c = w.shape
    assert 0 < c - cols <= LANES
    return pl.pallas_call(
        _cast_split_kernel,
        grid=(nl, r // tr),
        in_specs=[pl.BlockSpec((None, tr, c), lambda l, i: (l, i, 0))],
        out_specs=[pl.BlockSpec((None, tr, cols), lambda l, i: (l, i, 0)),
                   pl.BlockSpec((None, tr, LANES), lambda l, i: (l, i, 0))],
        out_shape=[jax.ShapeDtypeStruct((nl, r, cols), BF16), jax.ShapeDtypeStruct((nl, r, LANES), BF16)],
        compiler_params=_params("parallel", "parallel"),
        name="cast_bf16_split",
    )(w)


def _pack_gate_up_kernel(g_ref, u_ref, o_ref):
    nf, _, tf2 = o_ref.shape
    tf = tf2 // 2
    for f in range(nf):
        o_ref[f, :, :tf] = g_ref[:, f * tf:(f + 1) * tf].astype(o_ref.dtype)
        o_ref[f, :, tf:] = u_ref[:, f * tf:(f + 1) * tf].astype(o_ref.dtype)


def pack_gate_up(w_gate, w_up, tf, *, tr):
    nl, d, dff = w_gate.shape
    nf = dff // tf
    spec = pl.BlockSpec((None, tr, dff), lambda l, i: (l, i, 0))
    return pl.pallas_call(
        _pack_gate_up_kernel,
        grid=(nl, d // tr),
        in_specs=[spec, spec],
        out_specs=pl.BlockSpec((None, nf, tr, 2 * tf), lambda l, i: (l, 0, i, 0)),
        out_shape=jax.ShapeDtypeStruct((nl, nf, d, 2 * tf), BF16),
        compiler_params=_params("parallel", "parallel"),
        name="pack_gate_up",
    )(w_gate, w_up)


def _norm_matmul_kernel(x_ref, g_ref, w_ref, *rest, has_extra, sub):
    if has_extra:
        w2_ref, o_ref, o2_ref = rest
    else:
        (o_ref,) = rest
    for r in range(x_ref.shape[0] // sub):
        rows = slice(r * sub, (r + 1) * sub)
        h = _rms(x_ref[rows, :], g_ref[...]).astype(BF16)
        o_ref[rows, :] = _dot(h, w_ref[...]).astype(o_ref.dtype)
        if has_extra:
            o2_ref[rows, :] = _dot(h, w2_ref[...]).astype(o2_ref.dtype)


def norm_matmul(x, g, w, w_extra=None, *, tm, layer=None):
    m, k = x.shape
    n = w.shape[-1]
    has_extra = w_extra is not None
    w_spec = _resident((k, n)) if layer is None else _resident_layer(w.shape, layer)
    in_specs = [pl.BlockSpec((tm, k), lambda i: (i, 0)), _resident((1, k)), w_spec]
    out_specs = [pl.BlockSpec((tm, n), lambda i: (i, 0))]
    out_shape = [jax.ShapeDtypeStruct((m, n), BF16)]
    args = [x, g.reshape(1, k), w]
    if has_extra:
        n2 = w_extra.shape[1]
        in_specs.append(_resident((k, n2)))
        out_specs.append(pl.BlockSpec((tm, n2), lambda i: (i, 0)))
        out_shape.append(jax.ShapeDtypeStruct((m, n2), BF16))
        args.append(w_extra)
    out = pl.pallas_call(
        functools.partial(_norm_matmul_kernel, has_extra=has_extra, sub=min(SUB_ROWS, tm)),
        grid=(m // tm,),
        in_specs=in_specs,
        out_specs=out_specs,
        out_shape=out_shape,
        compiler_params=_params("parallel"),
        name="norm_matmul",
    )(*args)
    return out if has_extra else out[0]


def _gated_matmul_norm_res_kernel(o_ref, r_ref, gn_ref, w_ref, g_ref, x_ref, out_ref, *, heads, sub):
    dv = o_ref.shape[1] // heads
    for r in range(x_ref.shape[0] // sub):
        rows = slice(r * sub, (r + 1) * sub)
        y = None
        for hd in range(heads):
            sl = slice(hd * dv, (hd + 1) * dv)
            a = _rms(o_ref[rows, sl].astype(F32), gn_ref[...]) * _silu(r_ref[rows, sl].astype(F32))
            part = _dot(a.astype(BF16), w_ref[sl, :])
            y = part if y is None else y + part
        out_ref[rows, :] = x_ref[rows, :] + _rms(y, g_ref[...])


def gated_matmul_norm_res(o, proj, r_block, gnorm, w, g, x, *, heads, tm, sub):
    m, n = x.shape
    kdim = o.shape[1]
    return pl.pallas_call(
        functools.partial(_gated_matmul_norm_res_kernel, heads=heads, sub=sub),
        grid=(m // tm,),
        in_specs=[
            pl.BlockSpec((tm, kdim), lambda i: (i, 0)),
            pl.BlockSpec((tm, kdim), lambda i: (i, r_block)),
            _resident((1, kdim // heads)),
            _resident(w.shape),
            _resident((1, n)),
            pl.BlockSpec((tm, n), lambda i: (i, 0)),
        ],
        out_specs=pl.BlockSpec((tm, n), lambda i: (i, 0)),
        out_shape=jax.ShapeDtypeStruct((m, n), F32),
        compiler_params=_params("parallel"),
        name="gated_matmul_norm_res",
    )(o, proj, gnorm.reshape(1, kdim // heads), w, g.reshape(1, n), x)


def _sb_kernel(q_ref, k_ref, v_ref, o_ref, acc_ref, after_ref, *, blk, nsub, scale):
    i = pl.program_id(2)
    rows = lax.broadcasted_iota(jnp.int32, (blk, blk), 0)
    cols = lax.broadcasted_iota(jnp.int32, (blk, blk), 1)
    later = (rows > cols).astype(BF16)
    later2 = jnp.concatenate([later, later], axis=0)

    def scaled_q(j):
        return (q_ref[j * blk:(j + 1) * blk, :].astype(F32) * scale).astype(BF16)

    def block(q, c, after, kind):
        start = pl.multiple_of(jnp.maximum(c, 0) * blk, blk)
        kc = k_ref[pl.ds(start, blk), :]
        vc = v_ref[pl.ds(start, blk), :]
        z = _dot_nt(q, kc)
        if kind == "diagonal":
            z = jnp.where(cols < rows, z, SB_MASK_VALUE)
        elif kind == "previous":
            z = jnp.where(c >= 0, z, SB_MASK_VALUE)
        smooth = _log1p_exp_neg_abs(z)
        log_stay = jnp.minimum(-z, 0.0) - smooth
        log_beta = jnp.minimum(z, 0.0) - smooth
        hi, lo = _split_bf16(log_stay)
        within = _dot(jnp.concatenate([hi, lo], axis=1), later2)
        w = jnp.exp(log_beta + within + after)
        return _dot(w.astype(BF16), vc), after + jnp.sum(log_stay, axis=-1, keepdims=True)

    for j in range(nsub):
        c = i * nsub + j
        q = scaled_q(j)
        acc_d, after_d = block(q, c, jnp.zeros((blk, 1), F32), "diagonal")
        acc_p, after_p = block(q, c - 1, after_d, "previous")
        acc_ref[j] = acc_d + acc_p
        after_ref[j] = after_p

    for j in range(nsub):
        q = scaled_q(j)

        def cond(c, j=j):
            return jnp.logical_and(c >= 0, jnp.max(after_ref[j]) > SB_LOG_UNDERFLOW)

        def body(c, j=j, q=q):
            acc_c, after_c = block(q, c, after_ref[j], "tail")
            acc_ref[j] += acc_c
            after_ref[j] = after_c
            return c - 1

        lax.while_loop(cond, body, i * nsub + j - 2)
        o_ref[j * blk:(j + 1) * blk, :] = acc_ref[j].astype(o_ref.dtype)


def stick_breaking_attention(proj, *, batch, seq, blk, nsub):
    t = proj.shape[0]
    tq = blk * nsub
    nq = seq // tq
    dh = SB_HEAD_DIM
    return pl.pallas_call(
        functools.partial(_sb_kernel, blk=blk, nsub=nsub, scale=dh ** -0.5),
        grid=(batch, SB_HEADS, nq),
        in_specs=[
            pl.BlockSpec((tq, dh), lambda b, h, i: (b * nq + i, h)),
            pl.BlockSpec((seq, dh), lambda b, h, i: (b, SB_HEADS + h)),
            pl.BlockSpec((seq, dh), lambda b, h, i: (b, 2 * SB_HEADS + h)),
        ],
        out_specs=pl.BlockSpec((tq, dh), lambda b, h, i: (b * nq + i, h)),
        out_shape=jax.ShapeDtypeStruct((t, SB_WIDTH), BF16),
        scratch_shapes=[pltpu.VMEM((nsub, blk, dh), F32), pltpu.VMEM((nsub, blk, 1), F32)],
        compiler_params=_params("parallel", "parallel", "arbitrary"),
        name="stick_breaking",
    )(proj, proj, proj)


def _pooled_group(u, halo, win, pos):
    ext = jnp.concatenate([halo, u], axis=0)
    s, span = ext, 1
    while span < win:
        s = s[span:] + s[:-span]
        span *= 2
    wsum = s[POOL_HALO - win + 1:]
    count = jnp.minimum(pos + 1, win).astype(F32)
    return wsum / count - u


def _even_out_kernel(sb_ref, u_ref, halo_ref, pw_ref, ps_ref, w_ref, g_ref, x_ref, o_ref,
                     *, sub, tiles_per_seq):
    tm = x_ref.shape[0]
    seq_tile = lax.rem(pl.program_id(0), tiles_per_seq)
    c = POOL_GROUP_DIM
    ksb = sb_ref.shape[1]
    for r in range(tm // sub):
        rows = slice(r * sub, (r + 1) * sub)
        if r == 0:
            halo = jnp.where(seq_tile > 0, halo_ref[...].astype(F32), 0.0)
        else:
            halo = u_ref[r * sub - POOL_HALO:r * sub, :].astype(F32)
        pos = seq_tile * tm + r * sub + lax.broadcasted_iota(jnp.int32, (sub, 1), 0)
        y = _dot(sb_ref[rows, :], w_ref[:ksb, :])
        for gi, win in enumerate(POOL_WINDOWS):
            cols = slice(gi * c, (gi + 1) * c)
            pooled = _pooled_group(u_ref[rows, cols].astype(F32), halo[:, cols], win, pos)
            mixed = (_dot(pooled.astype(BF16), pw_ref[gi]) * ps_ref[:, cols]).astype(BF16)
            y = y + _dot(mixed, w_ref[ksb + gi * c:ksb + (gi + 1) * c, :])
        o_ref[rows, :] = x_ref[rows, :] + _rms(y, g_ref[...])


def even_mixer_output(o_sb, proj, pool_w, pool_scale, w, g, x, *, seq, tm, sub):
    m, n = x.shape
    width = pool_scale.shape[0]
    ublk = proj.shape[1] // width - 1
    hb = tm // POOL_HALO
    return pl.pallas_call(
        functools.partial(_even_out_kernel, sub=sub, tiles_per_seq=seq // tm),
        grid=(m // tm,),
        in_specs=[
            pl.BlockSpec((tm, o_sb.shape[1]), lambda i: (i, 0)),
            pl.BlockSpec((tm, width), lambda i: (i, ublk)),
            pl.BlockSpec((POOL_HALO, width), lambda i: (jnp.maximum(i * hb - 1, 0), ublk)),
            _resident(pool_w.shape),
            _resident((1, width)),
            _resident(w.shape),
            _resident((1, n)),
            pl.BlockSpec((tm, n), lambda i: (i, 0)),
        ],
        out_specs=pl.BlockSpec((tm, n), lambda i: (i, 0)),
        out_shape=jax.ShapeDtypeStruct((m, n), F32),
        compiler_params=_params("parallel"),
        name="even_mixer_output",
    )(o_sb, proj, proj, pool_w, pool_scale.reshape(1, width), w, g.reshape(1, n), x)


def _gla_kernel(q_ref, k_ref, v_ref, a_ref, gw_ref, gb_ref, o_ref, state_ref,
                *, ts, chunk, nsc, dk, dv, scale):
    @pl.when(pl.program_id(2) == 0)
    def _():
        state_ref[...] = jnp.zeros_like(state_ref)

    heads = state_ref.shape[0]
    group = chunk * nsc
    gate_in = _dot(a_ref[...], gw_ref[...]) + gb_ref[...]
    g = (jnp.minimum(gate_in, 0.0) - _log1p_exp_neg_abs(gate_in)) * (1.0 / GLA_GATE_NORMALIZER)
    li = lax.broadcasted_iota(jnp.int32, (chunk, chunk), 0)
    mi = lax.broadcasted_iota(jnp.int32, (chunk, chunk), 1)
    causal = mi <= li
    causal_bf = causal.astype(BF16)
    causal2 = jnp.concatenate([causal_bf, causal_bf], axis=1)
    for c in range(ts // group):
        for hh in range(heads):
            kcols = slice(hh * dk, (hh + 1) * dk)
            vcols = slice(hh * dv, (hh + 1) * dv)
            state = state_ref[hh]
            q_loc, k_loc, k_loc_bf, b_end, v_sub, o_sub = [], [], [], [], [], []
            for si in range(nsc):
                sl = slice(c * group + si * chunk, c * group + (si + 1) * chunk)
                g_hi, g_lo = _split_bf16(g[sl, kcols])
                b = _dot(causal2, jnp.concatenate([g_hi, g_lo], axis=0))
                b_mid = b[chunk // 2:chunk // 2 + 1]
                b_last = b[chunk - 1:chunk]
                qc = q_ref[sl, kcols].astype(F32) * scale
                kc = k_ref[sl, kcols].astype(F32)
                vc = v_ref[sl, vcols]
                qi = (qc * jnp.exp(b - b_mid)).astype(BF16)
                ki = (kc * jnp.exp(b_mid - b)).astype(BF16)
                scores = jnp.where(causal, _dot_nt(qi, ki), 0.0).astype(BF16)
                o_sub.append(_dot(scores, vc))
                q_loc.append(qc * jnp.exp(b))
                k_loc.append(kc * jnp.exp(b_last - b))
                k_loc_bf.append(k_loc[si].astype(BF16))
                b_end.append(b_last)
                v_sub.append(vc)
            prefix = [None]
            for si in range(nsc):
                prefix.append(b_end[si] if prefix[si] is None else prefix[si] + b_end[si])
            total = prefix[nsc]
            q_inter, k_state = [], []
            for si in range(nsc):
                q_inter.append((q_loc[si] if si == 0 else q_loc[si] * jnp.exp(prefix[si])).astype(BF16))
                k_state.append(k_loc_bf[si] if si == nsc - 1
                               else (k_loc[si] * jnp.exp(total - prefix[si + 1])).astype(BF16))
                for sj in range(si):
                    qs = (q_loc[si] if sj == si - 1
                          else q_loc[si] * jnp.exp(prefix[si] - prefix[sj + 1])).astype(BF16)
                    cross = _dot_nt(qs, k_loc_bf[sj]).astype(BF16)
                    o_sub[si] = o_sub[si] + _dot(cross, v_sub[sj])
            inter = _dot(jnp.concatenate(q_inter, axis=0), state.astype(BF16))
            decay = jnp.broadcast_to(jnp.exp(total), (LANES, dk)).T[:, :1]
            state_ref[hh] = decay * state + _dot_tn(jnp.concatenate(k_state, axis=0),
                                                     jnp.concatenate(v_sub, axis=0))
            for si in range(nsc):
                sl = slice(c * group + si * chunk, c * group + (si + 1) * chunk)
                o_ref[sl, vcols] = (o_sub[si] + inter[si * chunk:(si + 1) * chunk]).astype(o_ref.dtype)


def gated_linear_attention(proj, a, gate_w, gate_b, *, batch, seq, ts):
    t = proj.shape[0]
    ns = seq // ts
    dk, dv, nh, hps = GLA_HK, GLA_HV, GLA_HEADS, GLA_HEADS_PER_STEP
    kw, vw = nh * dk, nh * dv
    ng = nh // hps
    bk, bv = hps * dk, hps * dv
    return pl.pallas_call(
        functools.partial(_gla_kernel, ts=ts, chunk=GLA_CHUNK, nsc=GLA_SUBCHUNKS, dk=dk, dv=dv,
                          scale=dk ** -0.5),
        grid=(batch, ng, ns),
        in_specs=[
            pl.BlockSpec((ts, bk), lambda b, h, s: (b * ns + s, h)),
            pl.BlockSpec((ts, bk), lambda b, h, s: (b * ns + s, ng + h)),
            pl.BlockSpec((ts, bv), lambda b, h, s: (b * ns + s, (2 * kw) // bv + h)),
            pl.BlockSpec((ts, LANES), lambda b, h, s: (b * ns + s, 0)),
            pl.BlockSpec((LANES, bk), lambda b, h, s: (0, h)),
            pl.BlockSpec((1, bk), lambda b, h, s: (0, h)),
        ],
        out_specs=pl.BlockSpec((ts, bv), lambda b, h, s: (b * ns + s, h)),
        out_shape=jax.ShapeDtypeStruct((t, vw), BF16),
        scratch_shapes=[pltpu.VMEM((hps, dk, dv), F32)],
        compiler_params=_params("parallel", "parallel", "arbitrary"),
        name="gated_linear_attention",
    )(proj, proj, proj, a, gate_w, gate_b.reshape(1, kw))


def _xattn_sublayer_kernel(x_ref, gpre_ref, wq_ref, k_ref, v_ref, wo_ref, gpost_ref, *rest,
                           heads, scale, sub, with_prep):
    if with_prep:
        gate_ref, up_ref, down_ref, in_ref, out_w_ref, o_ref, gu_dst, down_dst, in_dst, tail_dst, out_w_dst = rest
    else:
        (o_ref,) = rest
    dh = wq_ref.shape[1] // heads
    for r in range(x_ref.shape[0] // sub):
        rows = slice(r * sub, (r + 1) * sub)
        xr = x_ref[rows, :]
        h = _rms(xr, gpre_ref[...]).astype(BF16)
        q = _dot(h, wq_ref[...]).astype(BF16)
        y = None
        for hd in range(heads):
            sl = slice(hd * dh, (hd + 1) * dh)
            s = _dot_nt(q[:, sl], k_ref[:, sl]) * scale
            p = jnp.exp(s - jnp.max(s, axis=-1, keepdims=True))
            den = jnp.sum(p, axis=-1, keepdims=True)
            o_h = (_dot(p.astype(BF16), v_ref[:, sl]) / den).astype(BF16)
            part = _dot(o_h, wo_ref[sl, :])
            y = part if y is None else y + part
        o_ref[rows, :] = xr + _rms(y, gpost_ref[...])
    if with_prep:
        _pack_gate_up_kernel(gate_ref, up_ref, gu_dst)
        _cast_kernel(down_ref, down_dst)
        _cast_split_kernel(in_ref, in_dst, tail_dst)
        _cast_kernel(out_w_ref, out_w_dst)


def _prep_spec(shape, n_steps, step_of, cols=None, lead=()):
    nl, r, c = shape
    rps = (nl * r) // n_steps
    assert rps * n_steps == nl * r and r % rps == 0 and rps % 16 == 0
    per_layer = r // rps
    block = (None,) + tuple(lead) + (rps, c if cols is None else cols)
    zeros = (0,) * len(lead)
    return pl.BlockSpec(block, lambda *g: (step_of(*g) // per_layer,) + zeros + (step_of(*g) % per_layer, 0))


def cross_attention_sublayer(x, g_pre, wq, k, v, layer, wo, g_post, *, batch, seq, n_mem, tm, sub,
                             prep=None):
    t, d = x.shape
    ns = seq // tm
    n_steps = batch * ns
    in_specs = [
        pl.BlockSpec((tm, d), lambda b, s: (b * ns + s, 0)),
        _resident((1, d)),
        _resident_layer(wq.shape, layer),
        pl.BlockSpec((n_mem, d), lambda b, s: (b, 0), pipeline_mode=pl.Buffered(1)),
        pl.BlockSpec((n_mem, d), lambda b, s: (b, 0), pipeline_mode=pl.Buffered(1)),
        _resident_layer(wo.shape, layer),
        _resident((1, d)),
    ]
    out_specs = [pl.BlockSpec((tm, d), lambda b, s: (b * ns + s, 0))]
    out_shape = [jax.ShapeDtypeStruct((t, d), F32)]
    args = [x, g_pre.reshape(1, d), wq, k, v, wo, g_post.reshape(1, d)]
    if prep is not None:
        w_gate, w_up, w_down, w_in, w_out, cols, tf = prep
        step_of = lambda b, s: b * ns + s
        spec = functools.partial(_prep_spec, n_steps=n_steps, step_of=step_of)
        nl, dm, dff = w_gate.shape
        nf = dff // tf
        assert 0 < w_in.shape[2] - cols <= LANES
        in_specs += [spec(w_gate.shape), spec(w_up.shape), spec(w_down.shape), spec(w_in.shape),
                     spec(w_out.shape)]
        out_specs += [spec((nl, dm, 2 * tf), lead=(nf,)), spec(w_down.shape), spec(w_in.shape, cols=cols),
                      spec(w_in.shape, cols=LANES), spec(w_out.shape)]
        out_shape += [jax.ShapeDtypeStruct((nl, nf, dm, 2 * tf), BF16),
                      jax.ShapeDtypeStruct(w_down.shape, BF16),
                      jax.ShapeDtypeStruct(w_in.shape[:2] + (cols,), BF16),
                      jax.ShapeDtypeStruct(w_in.shape[:2] + (LANES,), BF16),
                      jax.ShapeDtypeStruct(w_out.shape, BF16)]
        args += [w_gate, w_up, w_down, w_in, w_out]
    out = pl.pallas_call(
        functools.partial(_xattn_sublayer_kernel, heads=X_HEADS, scale=(d // X_HEADS) ** -0.5,
                          sub=sub, with_prep=prep is not None),
        grid=(batch, ns),
        in_specs=in_specs,
        out_specs=out_specs,
        out_shape=out_shape,
        compiler_params=_params("parallel", "parallel"),
        name="cross_attention_sublayer",
    )(*args)
    return out if prep is not None else out[0]


def _ffn_kernel(x_ref, gpre_ref, wgu_ref, wd_ref, gpost_ref, o_ref, h_ref, *, sub):
    f = pl.program_id(1)
    last = pl.num_programs(1) - 1
    tf = wd_ref.shape[0]

    def step(first, final):
        for r in range(x_ref.shape[0] // sub):
            rows = slice(r * sub, (r + 1) * sub)
            if first:
                h = _rms(x_ref[rows, :], gpre_ref[...]).astype(BF16)
                h_ref[rows, :] = h
            else:
                h = h_ref[rows, :]
            gate_up = _dot(h, wgu_ref[...])
            act = (_silu(gate_up[:, :tf]) * gate_up[:, tf:]).astype(BF16)
            y = _dot(act, wd_ref[...])
            if not first:
                y = o_ref[rows, :] + y
            if final:
                o_ref[rows, :] = x_ref[rows, :] + _rms(y, gpost_ref[...])
            else:
                o_ref[rows, :] = y

    pl.when(f == 0)(lambda: step(True, False))
    pl.when(jnp.logical_and(f > 0, f < last))(lambda: step(False, False))
    pl.when(f == last)(lambda: step(False, True))


def swiglu_sublayer(x, g_pre, w_gate_up, w_down, layer, g_post, *, tm, sub):
    m, d = x.shape
    _, nf, _, tf2 = w_gate_up.shape
    tf = tf2 // 2
    assert nf >= 2
    return pl.pallas_call(
        functools.partial(_ffn_kernel, sub=sub),
        grid=(m // tm, nf),
        in_specs=[
            pl.BlockSpec((tm, d), lambda i, f: (i, 0)),
            _resident((1, d)),
            pl.BlockSpec((None, None, d, tf2), lambda i, f: (layer, f, 0, 0)),
            pl.BlockSpec((None, tf, d), lambda i, f: (layer, f, 0)),
            _resident((1, d)),
        ],
        out_specs=pl.BlockSpec((tm, d), lambda i, f: (i, 0)),
        out_shape=jax.ShapeDtypeStruct((m, d), F32),
        scratch_shapes=[pltpu.VMEM((tm, d), BF16)],
        compiler_params=_params("parallel", "arbitrary"),
        name="swiglu_sublayer",
    )(x, g_pre.reshape(1, d), w_gate_up, w_down, g_post.reshape(1, d))


def kernel(x, mem, norms, mem_norm, even_w_in, pool_w, pool_scale, even_w_out,
           odd_w_in, gla_gate_w, gla_gate_b, gla_gnorm, odd_w_out,
           xattn_wq, xattn_wk, xattn_wv, xattn_wo,
           ffn_w_gate, ffn_w_up, ffn_w_down):
    batch, seq, d = x.shape
    n_mem = mem.shape[1]
    depth = norms.shape[0]
    t = batch * seq
    bf = lambda w: w.astype(BF16)

    n_main = 2 * GLA_HEADS * GLA_HK + 2 * GLA_HEADS * GLA_HV
    wq, wo = cast_bf16(xattn_wq, tr=512), cast_bf16(xattn_wo, tr=512)
    w_in_even, w_out_even = cast_bf16(even_w_in, tr=256), cast_bf16(even_w_out, tr=512)
    wk, wv = cast_bf16(xattn_wk, tr=512), cast_bf16(xattn_wv, tr=512)
    later_weights = (ffn_w_gate, ffn_w_up, ffn_w_down, odd_w_in, odd_w_out, n_main, FFN_HIDDEN_TILE)

    xs = x.reshape(t, d)
    mem2 = mem.reshape(batch * n_mem, d)
    mem_k = [norm_matmul(mem2, mem_norm, wk, tm=batch * n_mem, layer=l) for l in range(depth)]
    mem_v = [norm_matmul(mem2, mem_norm, wv, tm=batch * n_mem, layer=l) for l in range(depth)]

    for layer in range(depth):
        g = norms[layer]
        i = layer // 2
        if layer % 2 == 0:
            proj = norm_matmul(xs, g[0], w_in_even[i], tm=1024)
            o_sb = stick_breaking_attention(proj, batch=batch, seq=seq, blk=256, nsub=4)
            xs = even_mixer_output(o_sb, proj, bf(pool_w[i]), pool_scale[i], w_out_even[i], g[1], xs,
                                   seq=seq, tm=1024, sub=512)
        else:
            gate_w = jnp.pad(bf(gla_gate_w[i]), ((0, LANES - GLA_GATE_RANK), (0, 0)))
            proj, a = norm_matmul(xs, g[0], w_in_odd[i], w_a_odd[i], tm=512)
            o = gated_linear_attention(proj, a, gate_w, gla_gate_b[i], batch=batch, seq=seq, ts=512)
            xs = gated_matmul_norm_res(o, proj, n_main // o.shape[1] - 1, gla_gnorm[i],
                                       w_out_odd[i], g[1], xs, heads=GLA_HEADS, tm=512, sub=256)
        out = cross_attention_sublayer(xs, g[2], wq, mem_k[layer], mem_v[layer], layer, wo, g[3],
                                       batch=batch, seq=seq, n_mem=n_mem, tm=512, sub=512,
                                       prep=later_weights if layer == 0 else None)
        if layer == 0:
            xs, w_gate_up, w_down, w_in_odd, w_a_odd, w_out_odd = out
        else:
            xs = out
        xs = swiglu_sublayer(xs, g[4], w_gate_up, w_down, layer, g[5], tm=1024, sub=512)
    return xs.reshape(batch, seq, d)
```

```python
import functools

import jax
import jax.numpy as jnp
from jax import lax
from jax.experimental import pallas as pl
from jax.experimental.pallas import tpu as pltpu

F32 = jnp.float32
BF16 = jnp.bfloat16

EPS = 1e-6

SB_HEADS = 4
SB_HEAD_DIM = 256
SB_WIDTH = SB_HEADS * SB_HEAD_DIM
SB_MASK_VALUE = -1e4
SB_LOG_UNDERFLOW = -88.0
POOL_WINDOWS = (2, 4, 8, 16)
POOL_HALO = 16
POOL_GROUP_DIM = 256

GLA_HEADS = 4
GLA_HK = 256
GLA_HV = 512
GLA_GATE_RANK = 16
GLA_GATE_NORMALIZER = 16.0
GLA_CHUNK = 64
GLA_SUBCHUNKS = 2
GLA_HEADS_PER_STEP = 2

X_HEADS = 4
FFN_HIDDEN_TILE = 512

V7X_VMEM_LIMIT_BYTES = 60 * 1024 * 1024
LANES = 128
LOG2E = 1.4426950408889634
SUB_ROWS = 256


def _params(*sem):
    return pltpu.CompilerParams(dimension_semantics=sem, vmem_limit_bytes=V7X_VMEM_LIMIT_BYTES)


def _resident(shape):
    zeros = (0,) * len(shape)
    return pl.BlockSpec(shape, lambda *_: zeros, pipeline_mode=pl.Buffered(1))


def _resident_layer(shape, layer):
    index = (layer,) + (0,) * (len(shape) - 1)
    return pl.BlockSpec((None,) + tuple(shape[1:]), lambda *_: index, pipeline_mode=pl.Buffered(1))


def _rms(x, g):
    ms = jnp.mean(x * x, axis=-1, keepdims=True)
    return x * lax.rsqrt(ms + EPS) * g


def _dot(a, b):
    return jnp.dot(a, b, preferred_element_type=F32)


def _dot_nt(a, b):
    return lax.dot_general(a, b, (((1,), (1,)), ((), ())), preferred_element_type=F32)


def _dot_tn(a, b):
    return lax.dot_general(a, b, (((0,), (0,)), ((), ())), preferred_element_type=F32)


def _split_bf16(x):
    hi = x.astype(BF16)
    lo = (x - hi.astype(F32)).astype(BF16)
    return hi, lo


def _silu(x):
    return x * jax.nn.sigmoid(x)


def _cast_kernel(w_ref, o_ref):
    o_ref[...] = w_ref[...].astype(o_ref.dtype)


def cast_bf16(w, *, tr, cols=None):
    nl, r, c = w.shape
    cols = c if cols is None else cols
    spec = pl.BlockSpec((None, tr, cols), lambda l, i: (l, i, 0))
    return pl.pallas_call(
        _cast_kernel,
        grid=(nl, r // tr),
        in_specs=[spec],
        out_specs=spec,
        out_shape=jax.ShapeDtypeStruct((nl, r, cols), BF16),
        compiler_params=_params("parallel", "parallel"),
        name="cast_bf16",
    )(w)


def _cast_split_kernel(w_ref, main_ref, tail_ref):
    cols = main_ref.shape[1]
    tail = w_ref.shape[1] - cols
    main_ref[...] = w_ref[:, :cols].astype(main_ref.dtype)
    tail_ref[...] = jnp.zeros_like(tail_ref)
    tail_ref[:, :tail] = w_ref[:, cols:].astype(tail_ref.dtype)


def _pack_gate_up_kernel(g_ref, u_ref, o_ref):
    nf, _, tf2 = o_ref.shape
    tf = tf2 // 2
    for f in range(nf):
        o_ref[f, :, :tf] = g_ref[:, f * tf:(f + 1) * tf].astype(o_ref.dtype)
        o_ref[f, :, tf:] = u_ref[:, f * tf:(f + 1) * tf].astype(o_ref.dtype)


def _norm_matmul_kernel(x_ref, g_ref, w_ref, *rest, has_extra, n_casts, sub):
    n_w2 = 1 if has_extra else 0
    w2_refs, src_refs = rest[:n_w2], rest[n_w2:n_w2 + n_casts]
    o_ref, o2_refs, dst_refs = rest[n_w2 + n_casts], rest[n_w2 + n_casts + 1:][:n_w2], rest[2 * n_w2 + n_casts + 1:]
    for r in range(x_ref.shape[0] // sub):
        rows = slice(r * sub, (r + 1) * sub)
        h = _rms(x_ref[rows, :], g_ref[...]).astype(BF16)
        o_ref[rows, :] = _dot(h, w_ref[...]).astype(o_ref.dtype)
        if has_extra:
            o2_refs[0][rows, :] = _dot(h, w2_refs[0][...]).astype(o2_refs[0].dtype)
    for src_ref, dst_ref in zip(src_refs, dst_refs):
        _cast_kernel(src_ref, dst_ref)


def norm_matmul(x, g, w, w_extra=None, *, tm, side_casts=()):
    m, k = x.shape
    n = w.shape[1]
    n_steps = m // tm
    has_extra = w_extra is not None
    in_specs = [pl.BlockSpec((tm, k), lambda i: (i, 0)), _resident((1, k)), _resident((k, n))]
    out_specs = [pl.BlockSpec((tm, n), lambda i: (i, 0))]
    out_shape = [jax.ShapeDtypeStruct((m, n), BF16)]
    args = [x, g.reshape(1, k), w]
    if has_extra:
        n2 = w_extra.shape[1]
        in_specs.append(_resident((k, n2)))
        out_specs.append(pl.BlockSpec((tm, n2), lambda i: (i, 0)))
        out_shape.append(jax.ShapeDtypeStruct((m, n2), BF16))
        args.append(w_extra)
    for wc in side_casts:
        spec = _prep_spec(wc.shape, n_steps, lambda i: i)
        in_specs.append(spec)
        out_specs.append(spec)
        out_shape.append(jax.ShapeDtypeStruct(wc.shape, BF16))
        args.append(wc)
    out = pl.pallas_call(
        functools.partial(_norm_matmul_kernel, has_extra=has_extra, n_casts=len(side_casts),
                          sub=min(SUB_ROWS, tm)),
        grid=(n_steps,),
        in_specs=in_specs,
        out_specs=out_specs,
        out_shape=out_shape,
        compiler_params=_params("parallel"),
        name="norm_matmul",
    )(*args)
    return out if len(out) > 1 else out[0]


def _memory_projection_kernel(mem_ref, g_ref, w_ref, o_ref):
    h = _rms(mem_ref[...], g_ref[...]).astype(BF16)
    o_ref[...] = _dot(h, w_ref[...].astype(BF16)).astype(o_ref.dtype)


def memory_projection(mem, g, w):
    m, d = mem.shape
    nl, _, n = w.shape
    return pl.pallas_call(
        _memory_projection_kernel,
        grid=(nl,),
        in_specs=[_resident((m, d)), _resident((1, d)), pl.BlockSpec((None, d, n), lambda l: (l, 0, 0))],
        out_specs=pl.BlockSpec((None, m, n), lambda l: (l, 0, 0)),
        out_shape=jax.ShapeDtypeStruct((nl, m, n), BF16),
        compiler_params=_params("parallel"),
        name="memory_projection",
    )(mem, g.reshape(1, d), w)


def _gated_matmul_norm_res_kernel(o_ref, r_ref, gn_ref, w_ref, g_ref, x_ref, out_ref, *, heads, sub):
    dv = o_ref.shape[1] // heads
    for r in range(x_ref.shape[0] // sub):
        rows = slice(r * sub, (r + 1) * sub)
        y = None
        for hd in range(heads):
            sl = slice(hd * dv, (hd + 1) * dv)
            a = _rms(o_ref[rows, sl].astype(F32), gn_ref[...]) * _silu(r_ref[rows, sl].astype(F32))
            part = _dot(a.astype(BF16), w_ref[sl, :])
            y = part if y is None else y + part
        out_ref[rows, :] = x_ref[rows, :] + _rms(y, g_ref[...])


def gated_matmul_norm_res(o, proj, r_block, gnorm, w, g, x, *, heads, tm, sub):
    m, n = x.shape
    kdim = o.shape[1]
    return pl.pallas_call(
        functools.partial(_gated_matmul_norm_res_kernel, heads=heads, sub=sub),
        grid=(m // tm,),
        in_specs=[
            pl.BlockSpec((tm, kdim), lambda i: (i, 0)),
            pl.BlockSpec((tm, kdim), lambda i: (i, r_block)),
            _resident((1, kdim // heads)),
            _resident(w.shape),
            _resident((1, n)),
            pl.BlockSpec((tm, n), lambda i: (i, 0)),
        ],
        out_specs=pl.BlockSpec((tm, n), lambda i: (i, 0)),
        out_shape=jax.ShapeDtypeStruct((m, n), F32),
        compiler_params=_params("parallel"),
        name="gated_matmul_norm_res",
    )(o, proj, gnorm.reshape(1, kdim // heads), w, g.reshape(1, n), x)


def _sb_kernel(q_ref, k_ref, v_ref, o_ref, acc_ref, after_ref, *, blk, nsub, scale):
    i = pl.program_id(2)
    rows = lax.broadcasted_iota(jnp.int32, (blk, blk), 0)
    cols = lax.broadcasted_iota(jnp.int32, (blk, blk), 1)
    later = (rows > cols).astype(BF16)
    later2 = jnp.concatenate([later, later], axis=0)

    def scaled_q(j):
        return (q_ref[j * blk:(j + 1) * blk, :].astype(F32) * (scale * LOG2E)).astype(BF16)

    def block(q, c, after, kind):
        start = pl.multiple_of(jnp.maximum(c, 0) * blk, blk)
        kc = k_ref[pl.ds(start, blk), :]
        vc = v_ref[pl.ds(start, blk), :]
        z = _dot_nt(q, kc)
        if kind == "diagonal":
            z = jnp.where(cols < rows, z, SB_MASK_VALUE * LOG2E)
        elif kind == "previous":
            z = jnp.where(c >= 0, z, SB_MASK_VALUE * LOG2E)
        neg_z = -z
        smooth = jnp.log(1.0 + jnp.exp2(jnp.minimum(z, neg_z))) * LOG2E
        log_stay = jnp.minimum(neg_z, 0.0) - smooth
        log_beta = log_stay + z
        hi, lo = _split_bf16(log_stay)
        within = _dot(jnp.concatenate([hi, lo], axis=1), later2)
        w = jnp.exp2(log_beta + within + after)
        return _dot(w.astype(BF16), vc), after + jnp.sum(log_stay, axis=-1, keepdims=True)

    for j in range(nsub):
        c = i * nsub + j
        q = scaled_q(j)
        acc_d, after_d = block(q, c, jnp.zeros((blk, 1), F32), "diagonal")
        acc_p, after_p = block(q, c - 1, after_d, "previous")
        acc_ref[j] = acc_d + acc_p
        after_ref[j] = after_p

    for j in range(nsub):
        q = scaled_q(j)

        def cond(c, j=j):
            return jnp.logical_and(c >= 0, jnp.max(after_ref[j]) > SB_LOG_UNDERFLOW * LOG2E)

        def body(c, j=j, q=q):
            acc_c, after_c = block(q, c, after_ref[j], "tail")
            acc_ref[j] += acc_c
            after_ref[j] = after_c
            return c - 1

        lax.while_loop(cond, body, i * nsub + j - 2)
        o_ref[j * blk:(j + 1) * blk, :] = acc_ref[j].astype(o_ref.dtype)


def stick_breaking_attention(proj, *, batch, seq, blk, nsub):
    t = proj.shape[0]
    tq = blk * nsub
    assert seq % tq == 0
    nq = seq // tq
    dh = SB_HEAD_DIM
    return pl.pallas_call(
        functools.partial(_sb_kernel, blk=blk, nsub=nsub, scale=dh ** -0.5),
        grid=(batch, SB_HEADS, nq),
        in_specs=[
            pl.BlockSpec((tq, dh), lambda b, h, i: (b * nq + i, h)),
            pl.BlockSpec((seq, dh), lambda b, h, i: (b, SB_HEADS + h)),
            pl.BlockSpec((seq, dh), lambda b, h, i: (b, 2 * SB_HEADS + h)),
        ],
        out_specs=pl.BlockSpec((tq, dh), lambda b, h, i: (b * nq + i, h)),
        out_shape=jax.ShapeDtypeStruct((t, SB_WIDTH), BF16),
        scratch_shapes=[pltpu.VMEM((nsub, blk, dh), F32), pltpu.VMEM((nsub, blk, 1), F32)],
        compiler_params=_params("parallel", "parallel", "arbitrary"),
        name="stick_breaking",
    )(proj, proj, proj)


def _pooled_group(u, halo, win, pos):
    ext = jnp.concatenate([halo, u], axis=0)
    s, span = ext, 1
    while span < win:
        s = s[span:] + s[:-span]
        span *= 2
    wsum = s[POOL_HALO - win + 1:]
    count = jnp.minimum(pos + 1, win).astype(F32)
    return wsum / count - u


def _even_out_kernel(sb_ref, u_ref, halo_ref, pw_ref, ps_ref, w_ref, g_ref, x_ref, o_ref,
                     *, sub, tiles_per_seq):
    tm = x_ref.shape[0]
    seq_tile = lax.rem(pl.program_id(0), tiles_per_seq)
    c = POOL_GROUP_DIM
    ksb = sb_ref.shape[1]
    for r in range(tm // sub):
        rows = slice(r * sub, (r + 1) * sub)
        if r == 0:
            halo = jnp.where(seq_tile > 0, halo_ref[...].astype(F32), 0.0)
        else:
            halo = u_ref[r * sub - POOL_HALO:r * sub, :].astype(F32)
        pos = seq_tile * tm + r * sub + lax.broadcasted_iota(jnp.int32, (sub, 1), 0)
        y = _dot(sb_ref[rows, :], w_ref[:ksb, :])
        for gi, win in enumerate(POOL_WINDOWS):
            cols = slice(gi * c, (gi + 1) * c)
            pooled = _pooled_group(u_ref[rows, cols].astype(F32), halo[:, cols], win, pos)
            mixed = (_dot(pooled.astype(BF16), pw_ref[gi]) * ps_ref[:, cols]).astype(BF16)
            y = y + _dot(mixed, w_ref[ksb + gi * c:ksb + (gi + 1) * c, :])
        o_ref[rows, :] = x_ref[rows, :] + _rms(y, g_ref[...])


def even_mixer_output(o_sb, proj, pool_w, pool_scale, w, g, x, *, seq, tm, sub):
    m, n = x.shape
    width = pool_scale.shape[0]
    ublk = proj.shape[1] // width - 1
    hb = tm // POOL_HALO
    return pl.pallas_call(
        functools.partial(_even_out_kernel, sub=sub, tiles_per_seq=seq // tm),
        grid=(m // tm,),
        in_specs=[
            pl.BlockSpec((tm, o_sb.shape[1]), lambda i: (i, 0)),
            pl.BlockSpec((tm, width), lambda i: (i, ublk)),
            pl.BlockSpec((POOL_HALO, width), lambda i: (jnp.maximum(i * hb - 1, 0), ublk)),
            _resident(pool_w.shape),
            _resident((1, width)),
            _resident(w.shape),
            _resident((1, n)),
            pl.BlockSpec((tm, n), lambda i: (i, 0)),
        ],
        out_specs=pl.BlockSpec((tm, n), lambda i: (i, 0)),
        out_shape=jax.ShapeDtypeStruct((m, n), F32),
        compiler_params=_params("parallel"),
        name="even_mixer_output",
    )(o_sb, proj, proj, pool_w, pool_scale.reshape(1, width), w, g.reshape(1, n), x)


def _gla_kernel(q_ref, k_ref, v_ref, a_ref, gw_ref, gb_ref, o_ref, state_ref,
                *, ts, chunk, nsc, dk, dv, scale):
    @pl.when(pl.program_id(2) == 0)
    def _():
        state_ref[...] = jnp.zeros_like(state_ref)

    heads = state_ref.shape[0]
    group = chunk * nsc
    gate_in = (_dot(a_ref[...], gw_ref[...]) + gb_ref[...]) * LOG2E
    smooth = jnp.log(1.0 + jnp.exp2(jnp.minimum(gate_in, -gate_in))) * LOG2E
    g = (jnp.minimum(gate_in, 0.0) - smooth) * (1.0 / GLA_GATE_NORMALIZER)
    li = lax.broadcasted_iota(jnp.int32, (chunk, chunk), 0)
    mi = lax.broadcasted_iota(jnp.int32, (chunk, chunk), 1)
    causal = mi <= li
    causal_bf = causal.astype(BF16)
    causal2 = jnp.concatenate([causal_bf, causal_bf], axis=1)
    for c in range(ts // group):
        for hh in range(heads):
            kcols = slice(hh * dk, (hh + 1) * dk)
            vcols = slice(hh * dv, (hh + 1) * dv)
            state = state_ref[hh]
            q_loc, k_loc, k_loc_bf, b_end, v_sub, o_sub = [], [], [], [], [], []
            for si in range(nsc):
                sl = slice(c * group + si * chunk, c * group + (si + 1) * chunk)
                g_hi, g_lo = _split_bf16(g[sl, kcols])
                b = _dot(causal2, jnp.concatenate([g_hi, g_lo], axis=0))
                b_mid = b[chunk // 2:chunk // 2 + 1]
                b_last = b[chunk - 1:chunk]
                qc = q_ref[sl, kcols].astype(F32) * scale
                kc = k_ref[sl, kcols].astype(F32)
                vc = v_ref[sl, vcols]
                qi = (qc * jnp.exp2(b - b_mid)).astype(BF16)
                ki = (kc * jnp.exp2(b_mid - b)).astype(BF16)
                scores = jnp.where(causal, _dot_nt(qi, ki), 0.0).astype(BF16)
                o_sub.append(_dot(scores, vc))
                q_loc.append(qc * jnp.exp2(b))
                k_loc.append(kc * jnp.exp2(b_last - b))
                k_loc_bf.append(k_loc[si].astype(BF16))
                b_end.append(b_last)
                v_sub.append(vc)
            prefix = [None]
            for si in range(nsc):
                prefix.append(b_end[si] if prefix[si] is None else prefix[si] + b_end[si])
            total = prefix[nsc]
            q_inter, k_state = [], []
            for si in range(nsc):
                q_inter.append((q_loc[si] if si == 0 else q_loc[si] * jnp.exp2(prefix[si])).astype(BF16))
                k_state.append(k_loc_bf[si] if si == nsc - 1
                               else (k_loc[si] * jnp.exp2(total - prefix[si + 1])).astype(BF16))
                for sj in range(si):
                    qs = (q_loc[si] if sj == si - 1
                          else q_loc[si] * jnp.exp2(prefix[si] - prefix[sj + 1])).astype(BF16)
                    cross = _dot_nt(qs, k_loc_bf[sj]).astype(BF16)
                    o_sub[si] = o_sub[si] + _dot(cross, v_sub[sj])
            inter = _dot(jnp.concatenate(q_inter, axis=0), state.astype(BF16))
            decay = jnp.broadcast_to(jnp.exp2(total), (LANES, dk)).T[:, :1]
            state_ref[hh] = decay * state + _dot_tn(jnp.concatenate(k_state, axis=0),
                                                     jnp.concatenate(v_sub, axis=0))
            for si in range(nsc):
                sl = slice(c * group + si * chunk, c * group + (si + 1) * chunk)
                o_ref[sl, vcols] = (o_sub[si] + inter[si * chunk:(si + 1) * chunk]).astype(o_ref.dtype)


def gated_linear_attention(proj, a, gate_w, gate_b, *, batch, seq, ts):
    t = proj.shape[0]
    ns = seq // ts
    dk, dv, nh, hps = GLA_HK, GLA_HV, GLA_HEADS, GLA_HEADS_PER_STEP
    kw, vw = nh * dk, nh * dv
    ng = nh // hps
    bk, bv = hps * dk, hps * dv
    return pl.pallas_call(
        functools.partial(_gla_kernel, ts=ts, chunk=GLA_CHUNK, nsc=GLA_SUBCHUNKS, dk=dk, dv=dv,
                          scale=dk ** -0.5),
        grid=(batch, ng, ns),
        in_specs=[
            pl.BlockSpec((ts, bk), lambda b, h, s: (b * ns + s, h)),
            pl.BlockSpec((ts, bk), lambda b, h, s: (b * ns + s, ng + h)),
            pl.BlockSpec((ts, bv), lambda b, h, s: (b * ns + s, (2 * kw) // bv + h)),
            pl.BlockSpec((ts, LANES), lambda b, h, s: (b * ns + s, 0)),
            pl.BlockSpec((LANES, bk), lambda b, h, s: (0, h)),
            pl.BlockSpec((1, bk), lambda b, h, s: (0, h)),
        ],
        out_specs=pl.BlockSpec((ts, bv), lambda b, h, s: (b * ns + s, h)),
        out_shape=jax.ShapeDtypeStruct((t, vw), BF16),
        scratch_shapes=[pltpu.VMEM((hps, dk, dv), F32)],
        compiler_params=_params("parallel", "parallel", "arbitrary"),
        name="gated_linear_attention",
    )(proj, proj, proj, a, gate_w, gate_b.reshape(1, kw))


def _xattn_sublayer_kernel(x_ref, gpre_ref, wq_ref, k_ref, v_ref, wo_ref, gpost_ref, *rest,
                           heads, scale, sub, with_prep):
    if with_prep:
        gate_ref, up_ref, down_ref, in_ref, out_w_ref, o_ref, gu_dst, down_dst, in_dst, tail_dst, out_w_dst = rest
    else:
        (o_ref,) = rest
    dh = wq_ref.shape[1] // heads
    for r in range(x_ref.shape[0] // sub):
        rows = slice(r * sub, (r + 1) * sub)
        xr = x_ref[rows, :]
        h = _rms(xr, gpre_ref[...]).astype(BF16)
        q = _dot(h, wq_ref[...]).astype(BF16)
        y = None
        for hd in range(heads):
            sl = slice(hd * dh, (hd + 1) * dh)
            s = _dot_nt(q[:, sl], k_ref[:, sl]) * scale
            p = jnp.exp(s - jnp.max(s, axis=-1, keepdims=True))
            den = jnp.sum(p, axis=-1, keepdims=True)
            o_h = (_dot(p.astype(BF16), v_ref[:, sl]) / den).astype(BF16)
            part = _dot(o_h, wo_ref[sl, :])
            y = part if y is None else y + part
        o_ref[rows, :] = xr + _rms(y, gpost_ref[...])
    if with_prep:
        _pack_gate_up_kernel(gate_ref, up_ref, gu_dst)
        _cast_kernel(down_ref, down_dst)
        _cast_split_kernel(in_ref, in_dst, tail_dst)
        _cast_kernel(out_w_ref, out_w_dst)


def _prep_spec(shape, n_steps, step_of, cols=None, lead=()):
    nl, r, c = shape
    rps = (nl * r) // n_steps
    assert rps * n_steps == nl * r and r % rps == 0 and rps % 16 == 0
    per_layer = r // rps
    block = (None,) + tuple(lead) + (rps, c if cols is None else cols)
    zeros = (0,) * len(lead)
    return pl.BlockSpec(block, lambda *g: (step_of(*g) // per_layer,) + zeros + (step_of(*g) % per_layer, 0))


def cross_attention_sublayer(x, g_pre, wq, k, v, layer, wo, g_post, *, batch, seq, n_mem, tm, sub,
                             prep=None):
    t, d = x.shape
    ns = seq // tm
    n_steps = batch * ns
    in_specs = [
        pl.BlockSpec((tm, d), lambda b, s: (b * ns + s, 0)),
        _resident((1, d)),
        _resident_layer(wq.shape, layer),
        pl.BlockSpec((None, n_mem, d), lambda b, s: (layer, b, 0), pipeline_mode=pl.Buffered(1)),
        pl.BlockSpec((None, n_mem, d), lambda b, s: (layer, b, 0), pipeline_mode=pl.Buffered(1)),
        _resident_layer(wo.shape, layer),
        _resident((1, d)),
    ]
    out_specs = [pl.BlockSpec((tm, d), lambda b, s: (b * ns + s, 0))]
    out_shape = [jax.ShapeDtypeStruct((t, d), F32)]
    args = [x, g_pre.reshape(1, d), wq, k, v, wo, g_post.reshape(1, d)]
    if prep is not None:
        w_gate, w_up, w_down, w_in, w_out, cols, tf = prep
        step_of = lambda b, s: b * ns + s
        spec = functools.partial(_prep_spec, n_steps=n_steps, step_of=step_of)
        nl, dm, dff = w_gate.shape
        nf = dff // tf
        assert 0 < w_in.shape[2] - cols <= LANES
        in_specs += [spec(w_gate.shape), spec(w_up.shape), spec(w_down.shape), spec(w_in.shape),
                     spec(w_out.shape)]
        out_specs += [spec((nl, dm, 2 * tf), lead=(nf,)), spec(w_down.shape), spec(w_in.shape, cols=cols),
                      spec(w_in.shape, cols=LANES), spec(w_out.shape)]
        out_shape += [jax.ShapeDtypeStruct((nl, nf, dm, 2 * tf), BF16),
                      jax.ShapeDtypeStruct(w_down.shape, BF16),
                      jax.ShapeDtypeStruct(w_in.shape[:2] + (cols,), BF16),
                      jax.ShapeDtypeStruct(w_in.shape[:2] + (LANES,), BF16),
                      jax.ShapeDtypeStruct(w_out.shape, BF16)]
        args += [w_gate, w_up, w_down, w_in, w_out]
    out = pl.pallas_call(
        functools.partial(_xattn_sublayer_kernel, heads=X_HEADS, scale=(d // X_HEADS) ** -0.5,
                          sub=sub, with_prep=prep is not None),
        grid=(batch, ns),
        in_specs=in_specs,
        out_specs=out_specs,
        out_shape=out_shape,
        compiler_params=_params("parallel", "parallel"),
        name="cross_attention_sublayer",
    )(*args)
    return out if prep is not None else out[0]


def _ffn_kernel(x_ref, gpre_ref, wgu_ref, wd_ref, gpost_ref, o_ref, h_ref, *, sub):
    f = pl.program_id(1)
    last = pl.num_programs(1) - 1
    tf = wd_ref.shape[0]

    def step(first, final):
        for r in range(x_ref.shape[0] // sub):
            rows = slice(r * sub, (r + 1) * sub)
            if first:
                h = _rms(x_ref[rows, :], gpre_ref[...]).astype(BF16)
                h_ref[rows, :] = h
            else:
                h = h_ref[rows, :]
            gate_up = _dot(h, wgu_ref[...])
            act = (_silu(gate_up[:, :tf]) * gate_up[:, tf:]).astype(BF16)
            y = _dot(act, wd_ref[...])
            if not first:
                y = o_ref[rows, :] + y
            if final:
                o_ref[rows, :] = x_ref[rows, :] + _rms(y, gpost_ref[...])
            else:
                o_ref[rows, :] = y

    pl.when(f == 0)(lambda: step(True, False))
    pl.when(jnp.logical_and(f > 0, f < last))(lambda: step(False, False))
    pl.when(f == last)(lambda: step(False, True))


def swiglu_sublayer(x, g_pre, w_gate_up, w_down, layer, g_post, *, tm, sub):
    m, d = x.shape
    _, nf, _, tf2 = w_gate_up.shape
    tf = tf2 // 2
    assert nf >= 2
    return pl.pallas_call(
        functools.partial(_ffn_kernel, sub=sub),
        grid=(m // tm, nf),
        in_specs=[
            pl.BlockSpec((tm, d), lambda i, f: (i, 0)),
            _resident((1, d)),
            pl.BlockSpec((None, None, d, tf2), lambda i, f: (layer, f, 0, 0)),
            pl.BlockSpec((None, tf, d), lambda i, f: (layer, f, 0)),
            _resident((1, d)),
        ],
        out_specs=pl.BlockSpec((tm, d), lambda i, f: (i, 0)),
        out_shape=jax.ShapeDtypeStruct((m, d), F32),
        scratch_shapes=[pltpu.VMEM((tm, d), BF16)],
        compiler_params=_params("parallel", "arbitrary"),
        name="swiglu_sublayer",
    )(x, g_pre.reshape(1, d), w_gate_up, w_down, g_post.reshape(1, d))


def kernel(x, mem, norms, mem_norm, even_w_in, pool_w, pool_scale, even_w_out,
           odd_w_in, gla_gate_w, gla_gate_b, gla_gnorm, odd_w_out,
           xattn_wq, xattn_wk, xattn_wv, xattn_wo,
           ffn_w_gate, ffn_w_up, ffn_w_down):
    batch, seq, d = x.shape
    n_mem = mem.shape[1]
    depth = norms.shape[0]
    t = batch * seq
    bf = lambda w: w.astype(BF16)

    n_main = 2 * GLA_HEADS * GLA_HK + 2 * GLA_HEADS * GLA_HV
    w_in_even = cast_bf16(even_w_in, tr=256)
    later_weights = (ffn_w_gate, ffn_w_up, ffn_w_down, odd_w_in, odd_w_out, n_main, FFN_HIDDEN_TILE)

    xs = x.reshape(t, d)
    mem2 = mem.reshape(batch * n_mem, d)
    mem_k = memory_projection(mem2, mem_norm, xattn_wk)
    mem_v = memory_projection(mem2, mem_norm, xattn_wv)

    for layer in range(depth):
        g = norms[layer]
        i = layer // 2
        if layer % 2 == 0:
            if layer == 0:
                proj, wq, wo, w_out_even = norm_matmul(xs, g[0], w_in_even[i], tm=1024,
                                                       side_casts=(xattn_wq, xattn_wo, even_w_out))
            else:
                proj = norm_matmul(xs, g[0], w_in_even[i], tm=1024)
            o_sb = stick_breaking_attention(proj, batch=batch, seq=seq, blk=256, nsub=8)
            xs = even_mixer_output(o_sb, proj, bf(pool_w[i]), pool_scale[i], w_out_even[i], g[1], xs,
                                   seq=seq, tm=1024, sub=512)
        else:
            gate_w = jnp.pad(bf(gla_gate_w[i]), ((0, LANES - GLA_GATE_RANK), (0, 0)))
            proj, a = norm_matmul(xs, g[0], w_in_odd[i], w_a_odd[i], tm=512)
            o = gated_linear_attention(proj, a, gate_w, gla_gate_b[i], batch=batch, seq=seq, ts=512)
            xs = gated_matmul_norm_res(o, proj, n_main // o.shape[1] - 1, gla_gnorm[i],
                                       w_out_odd[i], g[1], xs, heads=GLA_HEADS, tm=512, sub=256)
        out = cross_attention_sublayer(xs, g[2], wq, mem_k, mem_v, layer, wo, g[3],
                                       batch=batch, seq=seq, n_mem=n_mem, tm=512, sub=512,
                                       prep=later_weights if layer == 0 else None)
        if layer == 0:
            xs, w_gate_up, w_down, w_in_odd, w_a_odd, w_out_odd = out
        else:
            xs = out
        xs = swiglu_sublayer(xs, g[4], w_gate_up, w_down, layer, g[5], tm=1024, sub=512)
    return xs.reshape(batch, seq, d)
```

```python
import functools
from typing import NamedTuple

import jax
import jax.numpy as jnp
from jax import lax
from jax.experimental import pallas as pl
from jax.experimental.pallas import tpu as pltpu

F32 = jnp.float32
BF16 = jnp.bfloat16

EPS = 1e-6

SB_HEADS = 4
SB_HEAD_DIM = 256
SB_WIDTH = SB_HEADS * SB_HEAD_DIM
SB_MASK_VALUE = -1e4
SB_LOG_UNDERFLOW = -88.0
POOL_WINDOWS = (2, 4, 8, 16)
POOL_HALO = 16
POOL_GROUP_DIM = 256

GLA_HEADS = 4
GLA_HK = 256
GLA_HV = 512
GLA_GATE_RANK = 16
GLA_GATE_NORMALIZER = 16.0
GLA_CHUNK = 64
GLA_SUBCHUNKS = 2
GLA_HEADS_PER_STEP = 2

X_HEADS = 4

V7X_VMEM_LIMIT_BYTES = 60 * 1024 * 1024
LANES = 128
LOG2E = 1.4426950408889634
SUB_ROWS = 256


class TilePlan(NamedTuple):
    even_in_rows: int = 1024
    odd_in_rows: int = 512
    even_out_rows: int = 1024
    even_out_dot_rows: int = 512
    odd_out_rows: int = 512
    odd_out_dot_rows: int = 256
    xattn_rows: int = 512
    ffn_rows: int = 1024
    ffn_dot_rows: int = 512
    ffn_hidden: int = 512
    sb_block: int = 256
    sb_blocks_per_step: int = 16
    gla_rows: int = 1024
    cast_rows: int = 256


TILE_PLAN = TilePlan()


def _params(*sem):
    return pltpu.CompilerParams(dimension_semantics=sem, vmem_limit_bytes=V7X_VMEM_LIMIT_BYTES)


def _resident(shape):
    zeros = (0,) * len(shape)
    return pl.BlockSpec(shape, lambda *_: zeros, pipeline_mode=pl.Buffered(1))


def _resident_layer(shape, layer):
    index = (layer,) + (0,) * (len(shape) - 1)
    return pl.BlockSpec((None,) + tuple(shape[1:]), lambda *_: index, pipeline_mode=pl.Buffered(1))


def _rms(x, g):
    ms = jnp.mean(x * x, axis=-1, keepdims=True)
    return x * lax.rsqrt(ms + EPS) * g


def _dot(a, b):
    return jnp.dot(a, b, preferred_element_type=F32)


def _dot_nt(a, b):
    return lax.dot_general(a, b, (((1,), (1,)), ((), ())), preferred_element_type=F32)


def _dot_tn(a, b):
    return lax.dot_general(a, b, (((0,), (0,)), ((), ())), preferred_element_type=F32)


def _split_bf16(x):
    hi = x.astype(BF16)
    lo = (x - hi.astype(F32)).astype(BF16)
    return hi, lo


def _silu(x):
    return x * jax.nn.sigmoid(x)


def _cast_kernel(w_ref, o_ref):
    o_ref[...] = w_ref[...].astype(o_ref.dtype)


def cast_bf16(w, *, tr, cols=None):
    nl, r, c = w.shape
    cols = c if cols is None else cols
    spec = pl.BlockSpec((None, tr, cols), lambda l, i: (l, i, 0))
    return pl.pallas_call(
        _cast_kernel,
        grid=(nl, r // tr),
        in_specs=[spec],
        out_specs=spec,
        out_shape=jax.ShapeDtypeStruct((nl, r, cols), BF16),
        compiler_params=_params("parallel", "parallel"),
        name="cast_bf16",
    )(w)


def _cast_split_kernel(w_ref, main_ref, tail_ref):
    cols = main_ref.shape[1]
    tail = w_ref.shape[1] - cols
    main_ref[...] = w_ref[:, :cols].astype(main_ref.dtype)
    tail_ref[...] = jnp.zeros_like(tail_ref)
    tail_ref[:, :tail] = w_ref[:, cols:].astype(tail_ref.dtype)


def _pack_gate_up_kernel(g_ref, u_ref, o_ref):
    nf, _, tf2 = o_ref.shape
    tf = tf2 // 2
    for f in range(nf):
        o_ref[f, :, :tf] = g_ref[:, f * tf:(f + 1) * tf].astype(o_ref.dtype)
        o_ref[f, :, tf:] = u_ref[:, f * tf:(f + 1) * tf].astype(o_ref.dtype)


def _norm_matmul_kernel(x_ref, g_ref, w_ref, *rest, has_extra, n_casts, sub):
    n_w2 = 1 if has_extra else 0
    w2_refs, src_refs = rest[:n_w2], rest[n_w2:n_w2 + n_casts]
    o_ref, o2_refs, dst_refs = rest[n_w2 + n_casts], rest[n_w2 + n_casts + 1:][:n_w2], rest[2 * n_w2 + n_casts + 1:]
    for r in range(x_ref.shape[0] // sub):
        rows = slice(r * sub, (r + 1) * sub)
        h = _rms(x_ref[rows, :], g_ref[...]).astype(BF16)
        o_ref[rows, :] = _dot(h, w_ref[...]).astype(o_ref.dtype)
        if has_extra:
            o2_refs[0][rows, :] = _dot(h, w2_refs[0][...]).astype(o2_refs[0].dtype)
    for src_ref, dst_ref in zip(src_refs, dst_refs):
        _cast_kernel(src_ref, dst_ref)


def norm_matmul(x, g, w, w_extra=None, *, tm, side_casts=()):
    m, k = x.shape
    n = w.shape[1]
    n_steps = m // tm
    has_extra = w_extra is not None
    in_specs = [pl.BlockSpec((tm, k), lambda i: (i, 0)), _resident((1, k)), _resident((k, n))]
    out_specs = [pl.BlockSpec((tm, n), lambda i: (i, 0))]
    out_shape = [jax.ShapeDtypeStruct((m, n), BF16)]
    args = [x, g.reshape(1, k), w]
    if has_extra:
        n2 = w_extra.shape[1]
        in_specs.append(_resident((k, n2)))
        out_specs.append(pl.BlockSpec((tm, n2), lambda i: (i, 0)))
        out_shape.append(jax.ShapeDtypeStruct((m, n2), BF16))
        args.append(w_extra)
    for wc in side_casts:
        spec = _prep_spec(wc.shape, n_steps, lambda i: i)
        in_specs.append(spec)
        out_specs.append(spec)
        out_shape.append(jax.ShapeDtypeStruct(wc.shape, BF16))
        args.append(wc)
    out = pl.pallas_call(
        functools.partial(_norm_matmul_kernel, has_extra=has_extra, n_casts=len(side_casts),
                          sub=min(SUB_ROWS, tm)),
        grid=(n_steps,),
        in_specs=in_specs,
        out_specs=out_specs,
        out_shape=out_shape,
        compiler_params=_params("parallel"),
        name="norm_matmul",
    )(*args)
    return out if len(out) > 1 else out[0]


def _memory_projection_kernel(mem_ref, g_ref, w_ref, o_ref):
    h = _rms(mem_ref[...], g_ref[...]).astype(BF16)
    o_ref[...] = _dot(h, w_ref[...].astype(BF16)).astype(o_ref.dtype)


def memory_projection(mem, g, w):
    m, d = mem.shape
    nl, _, n = w.shape
    return pl.pallas_call(
        _memory_projection_kernel,
        grid=(nl,),
        in_specs=[_resident((m, d)), _resident((1, d)), pl.BlockSpec((None, d, n), lambda l: (l, 0, 0))],
        out_specs=pl.BlockSpec((None, m, n), lambda l: (l, 0, 0)),
        out_shape=jax.ShapeDtypeStruct((nl, m, n), BF16),
        compiler_params=_params("parallel"),
        name="memory_projection",
    )(mem, g.reshape(1, d), w)


def _gated_matmul_norm_res_kernel(o_ref, r_ref, gn_ref, w_ref, g_ref, x_ref, out_ref, *, heads, sub):
    dv = o_ref.shape[1] // heads
    for r in range(x_ref.shape[0] // sub):
        rows = slice(r * sub, (r + 1) * sub)
        y = None
        for hd in range(heads):
            sl = slice(hd * dv, (hd + 1) * dv)
            a = _rms(o_ref[rows, sl].astype(F32), gn_ref[...]) * _silu(r_ref[rows, sl].astype(F32))
            part = _dot(a.astype(BF16), w_ref[sl, :])
            y = part if y is None else y + part
        out_ref[rows, :] = x_ref[rows, :] + _rms(y, g_ref[...])


def gated_matmul_norm_res(o, proj, r_block, gnorm, w, g, x, *, heads, tm, sub):
    m, n = x.shape
    kdim = o.shape[1]
    return pl.pallas_call(
        functools.partial(_gated_matmul_norm_res_kernel, heads=heads, sub=sub),
        grid=(m // tm,),
        in_specs=[
            pl.BlockSpec((tm, kdim), lambda i: (i, 0)),
            pl.BlockSpec((tm, kdim), lambda i: (i, r_block)),
            _resident((1, kdim // heads)),
            _resident(w.shape),
            _resident((1, n)),
            pl.BlockSpec((tm, n), lambda i: (i, 0)),
        ],
        out_specs=pl.BlockSpec((tm, n), lambda i: (i, 0)),
        out_shape=jax.ShapeDtypeStruct((m, n), F32),
        compiler_params=_params("parallel"),
        name="gated_matmul_norm_res",
    )(o, proj, gnorm.reshape(1, kdim // heads), w, g.reshape(1, n), x)


def _sb_kernel(q_ref, k_ref, v_ref, o_ref, acc_ref, after_ref, *, blk, nsub, scale):
    i = pl.program_id(2)
    rows = lax.broadcasted_iota(jnp.int32, (blk, blk), 0)
    cols = lax.broadcasted_iota(jnp.int32, (blk, blk), 1)
    later = (rows > cols).astype(BF16)
    later2 = jnp.concatenate([later, later], axis=0)

    def scaled_q(j):
        return (q_ref[j * blk:(j + 1) * blk, :].astype(F32) * (scale * LOG2E)).astype(BF16)

    def block(q, c, after, kind):
        start = pl.multiple_of(jnp.maximum(c, 0) * blk, blk)
        kc = k_ref[pl.ds(start, blk), :]
        vc = v_ref[pl.ds(start, blk), :]
        z = _dot_nt(q, kc)
        if kind == "diagonal":
            z = jnp.where(cols < rows, z, SB_MASK_VALUE * LOG2E)
        elif kind == "previous":
            z = jnp.where(c >= 0, z, SB_MASK_VALUE * LOG2E)
        neg_z = -z
        smooth = jnp.log(1.0 + jnp.exp2(jnp.minimum(z, neg_z))) * LOG2E
        log_stay = jnp.minimum(neg_z, 0.0) - smooth
        log_beta = log_stay + z
        hi, lo = _split_bf16(log_stay)
        within = _dot(jnp.concatenate([hi, lo], axis=1), later2)
        w = jnp.exp2(log_beta + within + after)
        return _dot(w.astype(BF16), vc), after + jnp.sum(log_stay, axis=-1, keepdims=True)

    for j in range(nsub):
        c = i * nsub + j
        q = scaled_q(j)
        acc_d, after_d = block(q, c, jnp.zeros((blk, 1), F32), "diagonal")
        acc_p, after_p = block(q, c - 1, after_d, "previous")
        acc_ref[j] = acc_d + acc_p
        after_ref[j] = after_p

    for j in range(nsub):
        q = scaled_q(j)

        def cond(c, j=j):
            return jnp.logical_and(c >= 0, jnp.max(after_ref[j]) > SB_LOG_UNDERFLOW * LOG2E)

        def body(c, j=j, q=q):
            acc_c, after_c = block(q, c, after_ref[j], "tail")
            acc_ref[j] += acc_c
            after_ref[j] = after_c
            return c - 1

        lax.while_loop(cond, body, i * nsub + j - 2)
        o_ref[j * blk:(j + 1) * blk, :] = acc_ref[j].astype(o_ref.dtype)


def stick_breaking_attention(proj, *, batch, seq, blk, nsub):
    t = proj.shape[0]
    tq = blk * nsub
    assert seq % tq == 0
    nq = seq // tq
    dh = SB_HEAD_DIM
    return pl.pallas_call(
        functools.partial(_sb_kernel, blk=blk, nsub=nsub, scale=dh ** -0.5),
        grid=(batch, SB_HEADS, nq),
        in_specs=[
            pl.BlockSpec((tq, dh), lambda b, h, i: (b * nq + i, h)),
            pl.BlockSpec((seq, dh), lambda b, h, i: (b, SB_HEADS + h)),
            pl.BlockSpec((seq, dh), lambda b, h, i: (b, 2 * SB_HEADS + h)),
        ],
        out_specs=pl.BlockSpec((tq, dh), lambda b, h, i: (b * nq + i, h)),
        out_shape=jax.ShapeDtypeStruct((t, SB_WIDTH), BF16),
        scratch_shapes=[pltpu.VMEM((nsub, blk, dh), F32), pltpu.VMEM((nsub, blk, 1), F32)],
        compiler_params=_params("parallel", "parallel", "arbitrary"),
        name="stick_breaking",
    )(proj, proj, proj)


def _pooled_group(u, halo, win, pos):
    ext = jnp.concatenate([halo, u], axis=0)
    s, span = ext, 1
    while span < win:
        s = s[span:] + s[:-span]
        span *= 2
    wsum = s[POOL_HALO - win + 1:]
    count = jnp.minimum(pos + 1, win).astype(F32)
    return wsum / count - u


def _even_out_kernel(sb_ref, u_ref, halo_ref, pw_ref, ps_ref, w_ref, g_ref, x_ref, o_ref,
                     *, sub, tiles_per_seq):
    tm = x_ref.shape[0]
    seq_tile = lax.rem(pl.program_id(0), tiles_per_seq)
    c = POOL_GROUP_DIM
    ksb = sb_ref.shape[1]
    for r in range(tm // sub):
        rows = slice(r * sub, (r + 1) * sub)
        if r == 0:
            halo = jnp.where(seq_tile > 0, halo_ref[...].astype(F32), 0.0)
        else:
            halo = u_ref[r * sub - POOL_HALO:r * sub, :].astype(F32)
        pos = seq_tile * tm + r * sub + lax.broadcasted_iota(jnp.int32, (sub, 1), 0)
        y = _dot(sb_ref[rows, :], w_ref[:ksb, :])
        for gi, win in enumerate(POOL_WINDOWS):
            cols = slice(gi * c, (gi + 1) * c)
            pooled = _pooled_group(u_ref[rows, cols].astype(F32), halo[:, cols], win, pos)
            mixed = (_dot(pooled.astype(BF16), pw_ref[gi]) * ps_ref[:, cols]).astype(BF16)
            y = y + _dot(mixed, w_ref[ksb + gi * c:ksb + (gi + 1) * c, :])
        o_ref[rows, :] = x_ref[rows, :] + _rms(y, g_ref[...])


def even_mixer_output(o_sb, proj, pool_w, pool_scale, w, g, x, *, seq, tm, sub):
    m, n = x.shape
    width = pool_scale.shape[0]
    ublk = proj.shape[1] // width - 1
    hb = tm // POOL_HALO
    return pl.pallas_call(
        functools.partial(_even_out_kernel, sub=sub, tiles_per_seq=seq // tm),
        grid=(m // tm,),
        in_specs=[
            pl.BlockSpec((tm, o_sb.shape[1]), lambda i: (i, 0)),
            pl.BlockSpec((tm, width), lambda i: (i, ublk)),
            pl.BlockSpec((POOL_HALO, width), lambda i: (jnp.maximum(i * hb - 1, 0), ublk)),
            _resident(pool_w.shape),
            _resident((1, width)),
            _resident(w.shape),
            _resident((1, n)),
            pl.BlockSpec((tm, n), lambda i: (i, 0)),
        ],
        out_specs=pl.BlockSpec((tm, n), lambda i: (i, 0)),
        out_shape=jax.ShapeDtypeStruct((m, n), F32),
        compiler_params=_params("parallel"),
        name="even_mixer_output",
    )(o_sb, proj, proj, pool_w, pool_scale.reshape(1, width), w, g.reshape(1, n), x)


def _gla_kernel(q_ref, k_ref, v_ref, a_ref, gw_ref, gb_ref, o_ref, state_ref,
                *, ts, chunk, nsc, dk, dv, scale):
    @pl.when(pl.program_id(2) == 0)
    def _():
        state_ref[...] = jnp.zeros_like(state_ref)

    heads = state_ref.shape[0]
    group = chunk * nsc
    gate_in = (_dot(a_ref[...], gw_ref[...]) + gb_ref[...]) * LOG2E
    smooth = jnp.log(1.0 + jnp.exp2(jnp.minimum(gate_in, -gate_in))) * LOG2E
    g = (jnp.minimum(gate_in, 0.0) - smooth) * (1.0 / GLA_GATE_NORMALIZER)
    li = lax.broadcasted_iota(jnp.int32, (chunk, chunk), 0)
    mi = lax.broadcasted_iota(jnp.int32, (chunk, chunk), 1)
    causal = mi <= li
    causal_bf = causal.astype(BF16)
    causal2 = jnp.concatenate([causal_bf, causal_bf], axis=1)
    for c in range(ts // group):
        for hh in range(heads):
            kcols = slice(hh * dk, (hh + 1) * dk)
            vcols = slice(hh * dv, (hh + 1) * dv)
            state = state_ref[hh]
            q_loc, k_loc, k_loc_bf, b_end, v_sub, o_sub = [], [], [], [], [], []
            for si in range(nsc):
                sl = slice(c * group + si * chunk, c * group + (si + 1) * chunk)
                g_hi, g_lo = _split_bf16(g[sl, kcols])
                b = _dot(causal2, jnp.concatenate([g_hi, g_lo], axis=0))
                b_mid = b[chunk // 2:chunk // 2 + 1]
                b_last = b[chunk - 1:chunk]
                qc = q_ref[sl, kcols].astype(F32) * scale
                kc = k_ref[sl, kcols].astype(F32)
                vc = v_ref[sl, vcols]
                qi = (qc * jnp.exp2(b - b_mid)).astype(BF16)
                ki = (kc * jnp.exp2(b_mid - b)).astype(BF16)
                scores = jnp.where(causal, _dot_nt(qi, ki), 0.0).astype(BF16)
                o_sub.append(_dot(scores, vc))
                q_loc.append(qc * jnp.exp2(b))
                k_loc.append(kc * jnp.exp2(b_last - b))
                k_loc_bf.append(k_loc[si].astype(BF16))
                b_end.append(b_last)
                v_sub.append(vc)
            prefix = [None]
            for si in range(nsc):
                prefix.append(b_end[si] if prefix[si] is None else prefix[si] + b_end[si])
            total = prefix[nsc]
            q_inter, k_state = [], []
            for si in range(nsc):
                q_inter.append((q_loc[si] if si == 0 else q_loc[si] * jnp.exp2(prefix[si])).astype(BF16))
                k_state.append(k_loc_bf[si] if si == nsc - 1
                               else (k_loc[si] * jnp.exp2(total - prefix[si + 1])).astype(BF16))
                for sj in range(si):
                    qs = (q_loc[si] if sj == si - 1
                          else q_loc[si] * jnp.exp2(prefix[si] - prefix[sj + 1])).astype(BF16)
                    cross = _dot_nt(qs, k_loc_bf[sj]).astype(BF16)
                    o_sub[si] = o_sub[si] + _dot(cross, v_sub[sj])
            inter = _dot(jnp.concatenate(q_inter, axis=0), state.astype(BF16))
            decay = jnp.broadcast_to(jnp.exp2(total), (LANES, dk)).T[:, :1]
            state_ref[hh] = decay * state + _dot_tn(jnp.concatenate(k_state, axis=0),
                                                     jnp.concatenate(v_sub, axis=0))
            for si in range(nsc):
                sl = slice(c * group + si * chunk, c * group + (si + 1) * chunk)
                o_ref[sl, vcols] = (o_sub[si] + inter[si * chunk:(si + 1) * chunk]).astype(o_ref.dtype)


def gated_linear_attention(proj, a, gate_w, gate_b, *, batch, seq, ts):
    t = proj.shape[0]
    ns = seq // ts
    dk, dv, nh, hps = GLA_HK, GLA_HV, GLA_HEADS, GLA_HEADS_PER_STEP
    kw, vw = nh * dk, nh * dv
    ng = nh // hps
    bk, bv = hps * dk, hps * dv
    return pl.pallas_call(
        functools.partial(_gla_kernel, ts=ts, chunk=GLA_CHUNK, nsc=GLA_SUBCHUNKS, dk=dk, dv=dv,
                          scale=dk ** -0.5),
        grid=(batch, ng, ns),
        in_specs=[
            pl.BlockSpec((ts, bk), lambda b, h, s: (b * ns + s, h)),
            pl.BlockSpec((ts, bk), lambda b, h, s: (b * ns + s, ng + h)),
            pl.BlockSpec((ts, bv), lambda b, h, s: (b * ns + s, (2 * kw) // bv + h)),
            pl.BlockSpec((ts, LANES), lambda b, h, s: (b * ns + s, 0)),
            pl.BlockSpec((LANES, bk), lambda b, h, s: (0, h)),
            pl.BlockSpec((1, bk), lambda b, h, s: (0, h)),
        ],
        out_specs=pl.BlockSpec((ts, bv), lambda b, h, s: (b * ns + s, h)),
        out_shape=jax.ShapeDtypeStruct((t, vw), BF16),
        scratch_shapes=[pltpu.VMEM((hps, dk, dv), F32)],
        compiler_params=_params("parallel", "parallel", "arbitrary"),
        name="gated_linear_attention",
    )(proj, proj, proj, a, gate_w, gate_b.reshape(1, kw))


def _xattn_sublayer_kernel(x_ref, gpre_ref, wq_ref, k_ref, v_ref, wo_ref, gpost_ref, *rest,
                           heads, scale, sub, with_prep):
    if with_prep:
        gate_ref, up_ref, down_ref, in_ref, out_w_ref, o_ref, gu_dst, down_dst, in_dst, tail_dst, out_w_dst = rest
    else:
        (o_ref,) = rest
    dh = wq_ref.shape[1] // heads
    for r in range(x_ref.shape[0] // sub):
        rows = slice(r * sub, (r + 1) * sub)
        xr = x_ref[rows, :]
        h = _rms(xr, gpre_ref[...]).astype(BF16)
        q = _dot(h, wq_ref[...]).astype(BF16)
        y = None
        for hd in range(heads):
            sl = slice(hd * dh, (hd + 1) * dh)
            s = _dot_nt(q[:, sl], k_ref[:, sl]) * scale
            p = jnp.exp(s - jnp.max(s, axis=-1, keepdims=True))
            den = jnp.sum(p, axis=-1, keepdims=True)
            o_h = (_dot(p.astype(BF16), v_ref[:, sl]) / den).astype(BF16)
            part = _dot(o_h, wo_ref[sl, :])
            y = part if y is None else y + part
        o_ref[rows, :] = xr + _rms(y, gpost_ref[...])
    if with_prep:
        _pack_gate_up_kernel(gate_ref, up_ref, gu_dst)
        _cast_kernel(down_ref, down_dst)
        _cast_split_kernel(in_ref, in_dst, tail_dst)
        _cast_kernel(out_w_ref, out_w_dst)


def _prep_spec(shape, n_steps, step_of, cols=None, lead=()):
    nl, r, c = shape
    rps = (nl * r) // n_steps
    assert rps * n_steps == nl * r and r % rps == 0 and rps % 16 == 0
    per_layer = r // rps
    block = (None,) + tuple(lead) + (rps, c if cols is None else cols)
    zeros = (0,) * len(lead)
    return pl.BlockSpec(block, lambda *g: (step_of(*g) // per_layer,) + zeros + (step_of(*g) % per_layer, 0))


def cross_attention_sublayer(x, g_pre, wq, k, v, layer, wo, g_post, *, batch, seq, n_mem, tm, sub,
                             prep=None):
    t, d = x.shape
    ns = seq // tm
    n_steps = batch * ns
    in_specs = [
        pl.BlockSpec((tm, d), lambda b, s: (b * ns + s, 0)),
        _resident((1, d)),
        _resident_layer(wq.shape, layer),
        pl.BlockSpec((None, n_mem, d), lambda b, s: (layer, b, 0), pipeline_mode=pl.Buffered(1)),
        pl.BlockSpec((None, n_mem, d), lambda b, s: (layer, b, 0), pipeline_mode=pl.Buffered(1)),
        _resident_layer(wo.shape, layer),
        _resident((1, d)),
    ]
    out_specs = [pl.BlockSpec((tm, d), lambda b, s: (b * ns + s, 0))]
    out_shape = [jax.ShapeDtypeStruct((t, d), F32)]
    args = [x, g_pre.reshape(1, d), wq, k, v, wo, g_post.reshape(1, d)]
    if prep is not None:
        w_gate, w_up, w_down, w_in, w_out, cols, tf = prep
        step_of = lambda b, s: b * ns + s
        spec = functools.partial(_prep_spec, n_steps=n_steps, step_of=step_of)
        nl, dm, dff = w_gate.shape
        nf = dff // tf
        assert 0 < w_in.shape[2] - cols <= LANES
        in_specs += [spec(w_gate.shape), spec(w_up.shape), spec(w_down.shape), spec(w_in.shape),
                     spec(w_out.shape)]
        out_specs += [spec((nl, dm, 2 * tf), lead=(nf,)), spec(w_down.shape), spec(w_in.shape, cols=cols),
                      spec(w_in.shape, cols=LANES), spec(w_out.shape)]
        out_shape += [jax.ShapeDtypeStruct((nl, nf, dm, 2 * tf), BF16),
                      jax.ShapeDtypeStruct(w_down.shape, BF16),
                      jax.ShapeDtypeStruct(w_in.shape[:2] + (cols,), BF16),
                      jax.ShapeDtypeStruct(w_in.shape[:2] + (LANES,), BF16),
                      jax.ShapeDtypeStruct(w_out.shape, BF16)]
        args += [w_gate, w_up, w_down, w_in, w_out]
    out = pl.pallas_call(
        functools.partial(_xattn_sublayer_kernel, heads=X_HEADS, scale=(d // X_HEADS) ** -0.5,
                          sub=sub, with_prep=prep is not None),
        grid=(batch, ns),
        in_specs=in_specs,
        out_specs=out_specs,
        out_shape=out_shape,
        compiler_params=_params("parallel", "parallel"),
        name="cross_attention_sublayer",
    )(*args)
    return out if prep is not None else out[0]


def _ffn_kernel(x_ref, gpre_ref, wgu_ref, wd_ref, gpost_ref, o_ref, h_ref, *, sub):
    f = pl.program_id(1)
    last = pl.num_programs(1) - 1
    tf = wd_ref.shape[0]

    def step(first, final):
        for r in range(x_ref.shape[0] // sub):
            rows = slice(r * sub, (r + 1) * sub)
            if first:
                h = _rms(x_ref[rows, :], gpre_ref[...]).astype(BF16)
                h_ref[rows, :] = h
            else:
                h = h_ref[rows, :]
            gate_up = _dot(h, wgu_ref[...])
            act = (_silu(gate_up[:, :tf]) * gate_up[:, tf:]).astype(BF16)
            y = _dot(act, wd_ref[...])
            if not first:
                y = o_ref[rows, :] + y
            if final:
                o_ref[rows, :] = x_ref[rows, :] + _rms(y, gpost_ref[...])
            else:
                o_ref[rows, :] = y

    pl.when(f == 0)(lambda: step(True, False))
    pl.when(jnp.logical_and(f > 0, f < last))(lambda: step(False, False))
    pl.when(f == last)(lambda: step(False, True))


def swiglu_sublayer(x, g_pre, w_gate_up, w_down, layer, g_post, *, tm, sub):
    m, d = x.shape
    _, nf, _, tf2 = w_gate_up.shape
    tf = tf2 // 2
    assert nf >= 2
    return pl.pallas_call(
        functools.partial(_ffn_kernel, sub=sub),
        grid=(m // tm, nf),
        in_specs=[
            pl.BlockSpec((tm, d), lambda i, f: (i, 0)),
            _resident((1, d)),
            pl.BlockSpec((None, None, d, tf2), lambda i, f: (layer, f, 0, 0)),
            pl.BlockSpec((None, tf, d), lambda i, f: (layer, f, 0)),
            _resident((1, d)),
        ],
        out_specs=pl.BlockSpec((tm, d), lambda i, f: (i, 0)),
        out_shape=jax.ShapeDtypeStruct((m, d), F32),
        scratch_shapes=[pltpu.VMEM((tm, d), BF16)],
        compiler_params=_params("parallel", "arbitrary"),
        name="swiglu_sublayer",
    )(x, g_pre.reshape(1, d), w_gate_up, w_down, g_post.reshape(1, d))


def kernel(x, mem, norms, mem_norm, even_w_in, pool_w, pool_scale, even_w_out,
           odd_w_in, gla_gate_w, gla_gate_b, gla_gnorm, odd_w_out,
           xattn_wq, xattn_wk, xattn_wv, xattn_wo,
           ffn_w_gate, ffn_w_up, ffn_w_down):
    batch, seq, d = x.shape
    n_mem = mem.shape[1]
    depth = norms.shape[0]
    t = batch * seq
    tp = TILE_PLAN
    bf = lambda w: w.astype(BF16)

    n_main = 2 * GLA_HEADS * GLA_HK + 2 * GLA_HEADS * GLA_HV
    w_in_even = cast_bf16(even_w_in, tr=tp.cast_rows)
    later_weights = (ffn_w_gate, ffn_w_up, ffn_w_down, odd_w_in, odd_w_out, n_main, tp.ffn_hidden)

    xs = x.reshape(t, d)
    mem2 = mem.reshape(batch * n_mem, d)
    mem_k = memory_projection(mem2, mem_norm, xattn_wk)
    mem_v = memory_projection(mem2, mem_norm, xattn_wv)

    for layer in range(depth):
        g = norms[layer]
        i = layer // 2
        if layer % 2 == 0:
            if layer == 0:
                proj, wq, wo, w_out_even = norm_matmul(xs, g[0], w_in_even[i], tm=tp.even_in_rows,
                                                       side_casts=(xattn_wq, xattn_wo, even_w_out))
            else:
                proj = norm_matmul(xs, g[0], w_in_even[i], tm=tp.even_in_rows)
            o_sb = stick_breaking_attention(proj, batch=batch, seq=seq, blk=tp.sb_block,
                                            nsub=tp.sb_blocks_per_step)
            xs = even_mixer_output(o_sb, proj, bf(pool_w[i]), pool_scale[i], w_out_even[i], g[1], xs,
                                   seq=seq, tm=tp.even_out_rows, sub=tp.even_out_dot_rows)
        else:
            gate_w = jnp.pad(bf(gla_gate_w[i]), ((0, LANES - GLA_GATE_RANK), (0, 0)))
            proj, a = norm_matmul(xs, g[0], w_in_odd[i], w_a_odd[i], tm=tp.odd_in_rows)
            o = gated_linear_attention(proj, a, gate_w, gla_gate_b[i], batch=batch, seq=seq,
                                       ts=tp.gla_rows)
            xs = gated_matmul_norm_res(o, proj, n_main // o.shape[1] - 1, gla_gnorm[i], w_out_odd[i],
                                       g[1], xs, heads=GLA_HEADS, tm=tp.odd_out_rows,
                                       sub=tp.odd_out_dot_rows)
        out = cross_attention_sublayer(xs, g[2], wq, mem_k, mem_v, layer, wo, g[3],
                                       batch=batch, seq=seq, n_mem=n_mem, tm=tp.xattn_rows,
                                       sub=tp.xattn_rows, prep=later_weights if layer == 0 else None)
        if layer == 0:
            xs, w_gate_up, w_down, w_in_odd, w_a_odd, w_out_odd = out
        else:
            xs = out
        xs = swiglu_sublayer(xs, g[4], w_gate_up, w_down, layer, g[5], tm=tp.ffn_rows,
                             sub=tp.ffn_dot_rows)
    return xs.reshape(batch, seq, d)
```

```python
import functools
from typing import NamedTuple

import jax
import jax.numpy as jnp
from jax import lax
from jax.experimental import pallas as pl
from jax.experimental.pallas import tpu as pltpu

F32 = jnp.float32
BF16 = jnp.bfloat16

EPS = 1e-6

SB_HEADS = 4
SB_HEAD_DIM = 256
SB_WIDTH = SB_HEADS * SB_HEAD_DIM
SB_MASK_VALUE = -1e4
SB_LOG_UNDERFLOW = -88.0
POOL_WINDOWS = (2, 4, 8, 16)
POOL_HALO = 16
POOL_GROUP_DIM = 256

GLA_HEADS = 4
GLA_HK = 256
GLA_HV = 512
GLA_GATE_RANK = 16
GLA_GATE_NORMALIZER = 16.0
GLA_CHUNK = 64
GLA_SUBCHUNKS = 2
GLA_HEADS_PER_STEP = 2

X_HEADS = 4

V7X_VMEM_LIMIT_BYTES = 60 * 1024 * 1024
LANES = 128
LOG2E = 1.4426950408889634
SUB_ROWS = 256


class TilePlan(NamedTuple):
    even_in_rows: int = 1024
    odd_in_rows: int = 512
    even_out_rows: int = 1024
    even_out_dot_rows: int = 512
    odd_out_rows: int = 512
    odd_out_dot_rows: int = 256
    xattn_rows: int = 512
    ffn_rows: int = 1024
    ffn_dot_rows: int = 512
    ffn_hidden: int = 512
    sb_block: int = 256
    sb_blocks_per_step: int = 16
    gla_rows: int = 1024
    cast_rows: int = 256


TILE_PLAN = TilePlan()


def _params(*sem):
    return pltpu.CompilerParams(dimension_semantics=sem, vmem_limit_bytes=V7X_VMEM_LIMIT_BYTES)


def _resident(shape):
    zeros = (0,) * len(shape)
    return pl.BlockSpec(shape, lambda *_: zeros, pipeline_mode=pl.Buffered(1))


def _resident_layer(shape, layer):
    index = (layer,) + (0,) * (len(shape) - 1)
    return pl.BlockSpec((None,) + tuple(shape[1:]), lambda *_: index, pipeline_mode=pl.Buffered(1))


def _rms(x, g):
    ms = jnp.mean(x * x, axis=-1, keepdims=True)
    return x * lax.rsqrt(ms + EPS) * g


def _dot(a, b):
    return jnp.dot(a, b, preferred_element_type=F32)


def _dot_nt(a, b):
    return lax.dot_general(a, b, (((1,), (1,)), ((), ())), preferred_element_type=F32)


def _dot_tn(a, b):
    return lax.dot_general(a, b, (((0,), (0,)), ((), ())), preferred_element_type=F32)


def _split_bf16(x):
    hi = x.astype(BF16)
    lo = (x - hi.astype(F32)).astype(BF16)
    return hi, lo


def _silu(x):
    return x * jax.nn.sigmoid(x)


def _cast_kernel(w_ref, o_ref):
    o_ref[...] = w_ref[...].astype(o_ref.dtype)


def cast_bf16(w, *, tr, cols=None):
    nl, r, c = w.shape
    cols = c if cols is None else cols
    spec = pl.BlockSpec((None, tr, cols), lambda l, i: (l, i, 0))
    return pl.pallas_call(
        _cast_kernel,
        grid=(nl, r // tr),
        in_specs=[spec],
        out_specs=spec,
        out_shape=jax.ShapeDtypeStruct((nl, r, cols), BF16),
        compiler_params=_params("parallel", "parallel"),
        name="cast_bf16",
    )(w)


def _cast_split_kernel(w_ref, main_ref, tail_ref):
    cols = main_ref.shape[1]
    tail = w_ref.shape[1] - cols
    main_ref[...] = w_ref[:, :cols].astype(main_ref.dtype)
    tail_ref[...] = jnp.zeros_like(tail_ref)
    tail_ref[:, :tail] = w_ref[:, cols:].astype(tail_ref.dtype)


def _pack_gate_up_kernel(g_ref, u_ref, o_ref):
    nf, _, tf2 = o_ref.shape
    tf = tf2 // 2
    for f in range(nf):
        o_ref[f, :, :tf] = g_ref[:, f * tf:(f + 1) * tf].astype(o_ref.dtype)
        o_ref[f, :, tf:] = u_ref[:, f * tf:(f + 1) * tf].astype(o_ref.dtype)


def _log2_forget_gate(gate_in):
    gate_in = gate_in * LOG2E
    smooth = jnp.log(1.0 + jnp.exp2(jnp.minimum(gate_in, -gate_in))) * LOG2E
    return (jnp.minimum(gate_in, 0.0) - smooth) * (1.0 / GLA_GATE_NORMALIZER)


def _norm_matmul_kernel(x_ref, g_ref, w_ref, *rest, has_extra, n_casts, sub, swish_from):
    n_w2 = 3 if has_extra else 0
    n_o2 = 1 if has_extra else 0
    w2_refs, src_refs = rest[:n_w2], rest[n_w2:n_w2 + n_casts]
    o_ref = rest[n_w2 + n_casts]
    o2_refs, dst_refs = rest[n_w2 + n_casts + 1:][:n_o2], rest[n_w2 + n_casts + 1 + n_o2:]
    for r in range(x_ref.shape[0] // sub):
        rows = slice(r * sub, (r + 1) * sub)
        h = _rms(x_ref[rows, :], g_ref[...]).astype(BF16)
        if has_extra:
            w2_ref, gw_ref, gb_ref = w2_refs
            a = _dot(h, w2_ref[...]).astype(BF16)
            o2_refs[0][rows, :] = _log2_forget_gate(_dot(a, gw_ref[...]) + gb_ref[...])
        if swish_from is None:
            o_ref[rows, :] = _dot(h, w_ref[...]).astype(o_ref.dtype)
        else:
            o_ref[rows, swish_from:] = _silu(_dot(h, w_ref[:, swish_from:])).astype(o_ref.dtype)
            o_ref[rows, :swish_from] = _dot(h, w_ref[:, :swish_from]).astype(o_ref.dtype)
    for src_ref, dst_ref in zip(src_refs, dst_refs):
        _cast_kernel(src_ref, dst_ref)


def norm_matmul(x, g, w, gate_head=None, *, tm, side_casts=(), swish_from=None):
    m, k = x.shape
    n = w.shape[1]
    n_steps = m // tm
    has_extra = gate_head is not None
    in_specs = [pl.BlockSpec((tm, k), lambda i: (i, 0)), _resident((1, k)), _resident((k, n))]
    out_specs = [pl.BlockSpec((tm, n), lambda i: (i, 0))]
    out_shape = [jax.ShapeDtypeStruct((m, n), BF16)]
    args = [x, g.reshape(1, k), w]
    if has_extra:
        w_a, gate_w, gate_b = gate_head
        n2 = gate_w.shape[1]
        in_specs += [_resident(w_a.shape), _resident(gate_w.shape), _resident((1, n2))]
        out_specs.append(pl.BlockSpec((tm, n2), lambda i: (i, 0)))
        out_shape.append(jax.ShapeDtypeStruct((m, n2), F32))
        args += [w_a, gate_w, gate_b.reshape(1, n2)]
    for wc in side_casts:
        spec = _prep_spec(wc.shape, n_steps, lambda i: i)
        in_specs.append(spec)
        out_specs.append(spec)
        out_shape.append(jax.ShapeDtypeStruct(wc.shape, BF16))
        args.append(wc)
    out = pl.pallas_call(
        functools.partial(_norm_matmul_kernel, has_extra=has_extra, n_casts=len(side_casts),
                          sub=min(SUB_ROWS, tm), swish_from=swish_from),
        grid=(n_steps,),
        in_specs=in_specs,
        out_specs=out_specs,
        out_shape=out_shape,
        compiler_params=_params("parallel"),
        name="norm_matmul",
    )(*args)
    return out if len(out) > 1 else out[0]


def _memory_projection_kernel(mem_ref, g_ref, w_ref, o_ref):
    h = _rms(mem_ref[...], g_ref[...]).astype(BF16)
    o_ref[...] = _dot(h, w_ref[...].astype(BF16)).astype(o_ref.dtype)


def memory_projection(mem, g, w):
    m, d = mem.shape
    nl, _, n = w.shape
    return pl.pallas_call(
        _memory_projection_kernel,
        grid=(nl,),
        in_specs=[_resident((m, d)), _resident((1, d)), pl.BlockSpec((None, d, n), lambda l: (l, 0, 0))],
        out_specs=pl.BlockSpec((None, m, n), lambda l: (l, 0, 0)),
        out_shape=jax.ShapeDtypeStruct((nl, m, n), BF16),
        compiler_params=_params("parallel"),
        name="memory_projection",
    )(mem, g.reshape(1, d), w)


def _gated_matmul_norm_res_kernel(o_ref, r_ref, gn_ref, w_ref, g_ref, x_ref, out_ref, *, heads, sub):
    dv = o_ref.shape[1] // heads
    for r in range(x_ref.shape[0] // sub):
        rows = slice(r * sub, (r + 1) * sub)
        y = None
        for hd in range(heads):
            sl = slice(hd * dv, (hd + 1) * dv)
            a = _rms(o_ref[rows, sl].astype(F32), gn_ref[...]) * r_ref[rows, sl].astype(F32)
            part = _dot(a.astype(BF16), w_ref[sl, :])
            y = part if y is None else y + part
        out_ref[rows, :] = x_ref[rows, :] + _rms(y, g_ref[...])


def gated_matmul_norm_res(o, proj, r_block, gnorm, w, g, x, *, heads, tm, sub):
    m, n = x.shape
    kdim = o.shape[1]
    return pl.pallas_call(
        functools.partial(_gated_matmul_norm_res_kernel, heads=heads, sub=sub),
        grid=(m // tm,),
        in_specs=[
            pl.BlockSpec((tm, kdim), lambda i: (i, 0)),
            pl.BlockSpec((tm, kdim), lambda i: (i, r_block)),
            _resident((1, kdim // heads)),
            _resident(w.shape),
            _resident((1, n)),
            pl.BlockSpec((tm, n), lambda i: (i, 0)),
        ],
        out_specs=pl.BlockSpec((tm, n), lambda i: (i, 0)),
        out_shape=jax.ShapeDtypeStruct((m, n), F32),
        compiler_params=_params("parallel"),
        name="gated_matmul_norm_res",
    )(o, proj, gnorm.reshape(1, kdim // heads), w, g.reshape(1, n), x)


def _sb_kernel(q_ref, k_ref, v_ref, o_ref, acc_ref, after_ref, *, blk, nsub, scale):
    i = pl.program_id(2)
    rows = lax.broadcasted_iota(jnp.int32, (blk, blk), 0)
    cols = lax.broadcasted_iota(jnp.int32, (blk, blk), 1)
    later = (rows > cols).astype(BF16)
    later2 = jnp.concatenate([later, later], axis=0)

    def scaled_q(j):
        return (q_ref[j * blk:(j + 1) * blk, :].astype(F32) * (scale * LOG2E)).astype(BF16)

    def block(q, c, after, kind):
        start = pl.multiple_of(jnp.maximum(c, 0) * blk, blk)
        kc = k_ref[pl.ds(start, blk), :]
        vc = v_ref[pl.ds(start, blk), :]
        z = _dot_nt(q, kc)
        if kind == "diagonal":
            z = jnp.where(cols < rows, z, SB_MASK_VALUE * LOG2E)
        elif kind == "previous":
            z = jnp.where(c >= 0, z, SB_MASK_VALUE * LOG2E)
        neg_z = -z
        smooth = jnp.log(1.0 + jnp.exp2(jnp.minimum(z, neg_z))) * LOG2E
        log_stay = jnp.minimum(neg_z, 0.0) - smooth
        log_beta = log_stay + z
        hi, lo = _split_bf16(log_stay)
        within = _dot(jnp.concatenate([hi, lo], axis=1), later2)
        w = jnp.exp2(log_beta + within + after)
        return _dot(w.astype(BF16), vc), after + jnp.sum(log_stay, axis=-1, keepdims=True)

    for j in range(nsub):
        c = i * nsub + j
        q = scaled_q(j)
        acc_d, after_d = block(q, c, jnp.zeros((blk, 1), F32), "diagonal")
        acc_p, after_p = block(q, c - 1, after_d, "previous")
        acc_ref[j] = acc_d + acc_p
        after_ref[j] = after_p

    for j in range(nsub):
        q = scaled_q(j)

        def cond(c, j=j):
            return jnp.logical_and(c >= 0, jnp.max(after_ref[j]) > SB_LOG_UNDERFLOW * LOG2E)

        def body(c, j=j, q=q):
            acc_c, after_c = block(q, c, after_ref[j], "tail")
            acc_ref[j] += acc_c
            after_ref[j] = after_c
            return c - 1

        lax.while_loop(cond, body, i * nsub + j - 2)
        o_ref[j * blk:(j + 1) * blk, :] = acc_ref[j].astype(o_ref.dtype)


def stick_breaking_attention(proj, *, batch, seq, blk, nsub):
    t = proj.shape[0]
    tq = blk * nsub
    assert seq % tq == 0
    nq = seq // tq
    dh = SB_HEAD_DIM
    return pl.pallas_call(
        functools.partial(_sb_kernel, blk=blk, nsub=nsub, scale=dh ** -0.5),
        grid=(batch, SB_HEADS, nq),
        in_specs=[
            pl.BlockSpec((tq, dh), lambda b, h, i: (b * nq + i, h)),
            pl.BlockSpec((seq, dh), lambda b, h, i: (b, SB_HEADS + h)),
            pl.BlockSpec((seq, dh), lambda b, h, i: (b, 2 * SB_HEADS + h)),
        ],
        out_specs=pl.BlockSpec((tq, dh), lambda b, h, i: (b * nq + i, h)),
        out_shape=jax.ShapeDtypeStruct((t, SB_WIDTH), BF16),
        scratch_shapes=[pltpu.VMEM((nsub, blk, dh), F32), pltpu.VMEM((nsub, blk, 1), F32)],
        compiler_params=_params("parallel", "parallel", "arbitrary"),
        name="stick_breaking",
    )(proj, proj, proj)


def _pooled_group(u, halo, win, pos):
    ext = jnp.concatenate([halo, u], axis=0)
    s, span = ext, 1
    while span < win:
        s = s[span:] + s[:-span]
        span *= 2
    wsum = s[POOL_HALO - win + 1:]
    count = jnp.minimum(pos + 1, win).astype(F32)
    return wsum / count - u


def _even_out_kernel(sb_ref, u_ref, halo_ref, pw_ref, ps_ref, w_ref, g_ref, x_ref, o_ref,
                     *, sub, tiles_per_seq):
    tm = x_ref.shape[0]
    seq_tile = lax.rem(pl.program_id(0), tiles_per_seq)
    c = POOL_GROUP_DIM
    ksb = sb_ref.shape[1]
    for r in range(tm // sub):
        rows = slice(r * sub, (r + 1) * sub)
        if r == 0:
            halo = jnp.where(seq_tile > 0, halo_ref[...].astype(F32), 0.0)
        else:
            halo = u_ref[r * sub - POOL_HALO:r * sub, :].astype(F32)
        pos = seq_tile * tm + r * sub + lax.broadcasted_iota(jnp.int32, (sub, 1), 0)
        y = _dot(sb_ref[rows, :], w_ref[:ksb, :])
        for gi, win in enumerate(POOL_WINDOWS):
            cols = slice(gi * c, (gi + 1) * c)
            pooled = _pooled_group(u_ref[rows, cols].astype(F32), halo[:, cols], win, pos)
            mixed = (_dot(pooled.astype(BF16), pw_ref[gi]) * ps_ref[:, cols]).astype(BF16)
            y = y + _dot(mixed, w_ref[ksb + gi * c:ksb + (gi + 1) * c, :])
        o_ref[rows, :] = x_ref[rows, :] + _rms(y, g_ref[...])


def even_mixer_output(o_sb, proj, pool_w, pool_scale, w, g, x, *, seq, tm, sub):
    m, n = x.shape
    width = pool_scale.shape[0]
    ublk = proj.shape[1] // width - 1
    hb = tm // POOL_HALO
    return pl.pallas_call(
        functools.partial(_even_out_kernel, sub=sub, tiles_per_seq=seq // tm),
        grid=(m // tm,),
        in_specs=[
            pl.BlockSpec((tm, o_sb.shape[1]), lambda i: (i, 0)),
            pl.BlockSpec((tm, width), lambda i: (i, ublk)),
            pl.BlockSpec((POOL_HALO, width), lambda i: (jnp.maximum(i * hb - 1, 0), ublk)),
            _resident(pool_w.shape),
            _resident((1, width)),
            _resident(w.shape),
            _resident((1, n)),
            pl.BlockSpec((tm, n), lambda i: (i, 0)),
        ],
        out_specs=pl.BlockSpec((tm, n), lambda i: (i, 0)),
        out_shape=jax.ShapeDtypeStruct((m, n), F32),
        compiler_params=_params("parallel"),
        name="even_mixer_output",
    )(o_sb, proj, proj, pool_w, pool_scale.reshape(1, width), w, g.reshape(1, n), x)


def _gla_kernel(q_ref, k_ref, v_ref, g_ref, o_ref, state_ref,
                *, ts, chunk, nsc, dk, dv, scale):
    @pl.when(pl.program_id(2) == 0)
    def _():
        state_ref[...] = jnp.zeros_like(state_ref)

    heads = state_ref.shape[0]
    group = chunk * nsc
    g = g_ref[...]
    li = lax.broadcasted_iota(jnp.int32, (chunk, chunk), 0)
    mi = lax.broadcasted_iota(jnp.int32, (chunk, chunk), 1)
    causal = mi <= li
    causal_bf = causal.astype(BF16)
    causal2 = jnp.concatenate([causal_bf, causal_bf], axis=1)
    for c in range(ts // group):
        for hh in range(heads):
            kcols = slice(hh * dk, (hh + 1) * dk)
            vcols = slice(hh * dv, (hh + 1) * dv)
            state = state_ref[hh]
            q_loc, k_loc, k_loc_bf, b_end, v_sub, o_sub = [], [], [], [], [], []
            for si in range(nsc):
                sl = slice(c * group + si * chunk, c * group + (si + 1) * chunk)
                g_hi, g_lo = _split_bf16(g[sl, kcols])
                b = _dot(causal2, jnp.concatenate([g_hi, g_lo], axis=0))
                b_mid = b[chunk // 2:chunk // 2 + 1]
                b_last = b[chunk - 1:chunk]
                qc = q_ref[sl, kcols].astype(F32) * scale
                kc = k_ref[sl, kcols].astype(F32)
                vc = v_ref[sl, vcols]
                qi = (qc * jnp.exp2(b - b_mid)).astype(BF16)
                ki = (kc * jnp.exp2(b_mid - b)).astype(BF16)
                scores = jnp.where(causal, _dot_nt(qi, ki), 0.0).astype(BF16)
                o_sub.append(_dot(scores, vc))
                q_loc.append(qc * jnp.exp2(b))
                k_loc.append(kc * jnp.exp2(b_last - b))
                k_loc_bf.append(k_loc[si].astype(BF16))
                b_end.append(b_last)
                v_sub.append(vc)
            prefix = [None]
            for si in range(nsc):
                prefix.append(b_end[si] if prefix[si] is None else prefix[si] + b_end[si])
            total = prefix[nsc]
            q_inter, k_state = [], []
            for si in range(nsc):
                q_inter.append((q_loc[si] if si == 0 else q_loc[si] * jnp.exp2(prefix[si])).astype(BF16))
                k_state.append(k_loc_bf[si] if si == nsc - 1
                               else (k_loc[si] * jnp.exp2(total - prefix[si + 1])).astype(BF16))
                for sj in range(si):
                    qs = (q_loc[si] if sj == si - 1
                          else q_loc[si] * jnp.exp2(prefix[si] - prefix[sj + 1])).astype(BF16)
                    cross = _dot_nt(qs, k_loc_bf[sj]).astype(BF16)
                    o_sub[si] = o_sub[si] + _dot(cross, v_sub[sj])
            inter = _dot(jnp.concatenate(q_inter, axis=0), state.astype(BF16))
            decay = jnp.broadcast_to(jnp.exp2(total), (LANES, dk)).T[:, :1]
            state_ref[hh] = decay * state + _dot_tn(jnp.concatenate(k_state, axis=0),
                                                     jnp.concatenate(v_sub, axis=0))
            for si in range(nsc):
                sl = slice(c * group + si * chunk, c * group + (si + 1) * chunk)
                o_ref[sl, vcols] = (o_sub[si] + inter[si * chunk:(si + 1) * chunk]).astype(o_ref.dtype)


def gated_linear_attention(proj, log_gate, *, batch, seq, ts):
    t = proj.shape[0]
    ns = seq // ts
    dk, dv, nh, hps = GLA_HK, GLA_HV, GLA_HEADS, GLA_HEADS_PER_STEP
    kw, vw = nh * dk, nh * dv
    ng = nh // hps
    bk, bv = hps * dk, hps * dv
    return pl.pallas_call(
        functools.partial(_gla_kernel, ts=ts, chunk=GLA_CHUNK, nsc=GLA_SUBCHUNKS, dk=dk, dv=dv,
                          scale=dk ** -0.5),
        grid=(batch, ng, ns),
        in_specs=[
            pl.BlockSpec((ts, bk), lambda b, h, s: (b * ns + s, h)),
            pl.BlockSpec((ts, bk), lambda b, h, s: (b * ns + s, ng + h)),
            pl.BlockSpec((ts, bv), lambda b, h, s: (b * ns + s, (2 * kw) // bv + h)),
            pl.BlockSpec((ts, bk), lambda b, h, s: (b * ns + s, h)),
        ],
        out_specs=pl.BlockSpec((ts, bv), lambda b, h, s: (b * ns + s, h)),
        out_shape=jax.ShapeDtypeStruct((t, vw), BF16),
        scratch_shapes=[pltpu.VMEM((hps, dk, dv), F32)],
        compiler_params=_params("parallel", "parallel", "arbitrary"),
        name="gated_linear_attention",
    )(proj, proj, proj, log_gate)


def _xattn_sublayer_kernel(x_ref, gpre_ref, wq_ref, k_ref, v_ref, wo_ref, gpost_ref, *rest,
                           heads, scale, sub, with_prep):
    if with_prep:
        gate_ref, up_ref, down_ref, in_ref, out_w_ref, o_ref, gu_dst, down_dst, in_dst, tail_dst, out_w_dst = rest
    else:
        (o_ref,) = rest
    dh = wq_ref.shape[1] // heads
    for r in range(x_ref.shape[0] // sub):
        rows = slice(r * sub, (r + 1) * sub)
        xr = x_ref[rows, :]
        h = _rms(xr, gpre_ref[...]).astype(BF16)
        q = _dot(h, wq_ref[...]).astype(BF16)
        y = None
        for hd in range(heads):
            sl = slice(hd * dh, (hd + 1) * dh)
            s = _dot_nt(q[:, sl], k_ref[:, sl]) * scale
            p = jnp.exp(s - jnp.max(s, axis=-1, keepdims=True))
            den = jnp.sum(p, axis=-1, keepdims=True)
            o_h = (_dot(p.astype(BF16), v_ref[:, sl]) / den).astype(BF16)
            part = _dot(o_h, wo_ref[sl, :])
            y = part if y is None else y + part
        o_ref[rows, :] = xr + _rms(y, gpost_ref[...])
    if with_prep:
        _pack_gate_up_kernel(gate_ref, up_ref, gu_dst)
        _cast_kernel(down_ref, down_dst)
        _cast_split_kernel(in_ref, in_dst, tail_dst)
        _cast_kernel(out_w_ref, out_w_dst)


def _prep_spec(shape, n_steps, step_of, cols=None, lead=()):
    nl, r, c = shape
    rps = (nl * r) // n_steps
    assert rps * n_steps == nl * r and r % rps == 0 and rps % 16 == 0
    per_layer = r // rps
    block = (None,) + tuple(lead) + (rps, c if cols is None else cols)
    zeros = (0,) * len(lead)
    return pl.BlockSpec(block, lambda *g: (step_of(*g) // per_layer,) + zeros + (step_of(*g) % per_layer, 0))


def cross_attention_sublayer(x, g_pre, wq, k, v, layer, wo, g_post, *, batch, seq, n_mem, tm, sub,
                             prep=None):
    t, d = x.shape
    ns = seq // tm
    n_steps = batch * ns
    in_specs = [
        pl.BlockSpec((tm, d), lambda b, s: (b * ns + s, 0)),
        _resident((1, d)),
        _resident_layer(wq.shape, layer),
        pl.BlockSpec((None, n_mem, d), lambda b, s: (layer, b, 0), pipeline_mode=pl.Buffered(1)),
        pl.BlockSpec((None, n_mem, d), lambda b, s: (layer, b, 0), pipeline_mode=pl.Buffered(1)),
        _resident_layer(wo.shape, layer),
        _resident((1, d)),
    ]
    out_specs = [pl.BlockSpec((tm, d), lambda b, s: (b * ns + s, 0))]
    out_shape = [jax.ShapeDtypeStruct((t, d), F32)]
    args = [x, g_pre.reshape(1, d), wq, k, v, wo, g_post.reshape(1, d)]
    if prep is not None:
        w_gate, w_up, w_down, w_in, w_out, cols, tf = prep
        step_of = lambda b, s: b * ns + s
        spec = functools.partial(_prep_spec, n_steps=n_steps, step_of=step_of)
        nl, dm, dff = w_gate.shape
        nf = dff // tf
        assert 0 < w_in.shape[2] - cols <= LANES
        in_specs += [spec(w_gate.shape), spec(w_up.shape), spec(w_down.shape), spec(w_in.shape),
                     spec(w_out.shape)]
        out_specs += [spec((nl, dm, 2 * tf), lead=(nf,)), spec(w_down.shape), spec(w_in.shape, cols=cols),
                      spec(w_in.shape, cols=LANES), spec(w_out.shape)]
        out_shape += [jax.ShapeDtypeStruct((nl, nf, dm, 2 * tf), BF16),
                      jax.ShapeDtypeStruct(w_down.shape, BF16),
                      jax.ShapeDtypeStruct(w_in.shape[:2] + (cols,), BF16),
                      jax.ShapeDtypeStruct(w_in.shape[:2] + (LANES,), BF16),
                      jax.ShapeDtypeStruct(w_out.shape, BF16)]
        args += [w_gate, w_up, w_down, w_in, w_out]
    out = pl.pallas_call(
        functools.partial(_xattn_sublayer_kernel, heads=X_HEADS, scale=(d // X_HEADS) ** -0.5,
                          sub=sub, with_prep=prep is not None),
        grid=(batch, ns),
        in_specs=in_specs,
        out_specs=out_specs,
        out_shape=out_shape,
        compiler_params=_params("parallel", "parallel"),
        name="cross_attention_sublayer",
    )(*args)
    return out if prep is not None else out[0]


def _ffn_kernel(x_ref, gpre_ref, wgu_ref, wd_ref, gpost_ref, o_ref, h_ref, *, sub):
    f = pl.program_id(1)
    last = pl.num_programs(1) - 1
    tf = wd_ref.shape[0]

    def step(first, final):
        for r in range(x_ref.shape[0] // sub):
            rows = slice(r * sub, (r + 1) * sub)
            if first:
                h = _rms(x_ref[rows, :], gpre_ref[...]).astype(BF16)
                h_ref[rows, :] = h
            else:
                h = h_ref[rows, :]
            gate_up = _dot(h, wgu_ref[...])
            act = (_silu(gate_up[:, :tf]) * gate_up[:, tf:]).astype(BF16)
            y = _dot(act, wd_ref[...])
            if not first:
                y = o_ref[rows, :] + y
            if final:
                o_ref[rows, :] = x_ref[rows, :] + _rms(y, gpost_ref[...])
            else:
                o_ref[rows, :] = y

    pl.when(f == 0)(lambda: step(True, False))
    pl.when(jnp.logical_and(f > 0, f < last))(lambda: step(False, False))
    pl.when(f == last)(lambda: step(False, True))


def swiglu_sublayer(x, g_pre, w_gate_up, w_down, layer, g_post, *, tm, sub):
    m, d = x.shape
    _, nf, _, tf2 = w_gate_up.shape
    tf = tf2 // 2
    assert nf >= 2
    return pl.pallas_call(
        functools.partial(_ffn_kernel, sub=sub),
        grid=(m // tm, nf),
        in_specs=[
            pl.BlockSpec((tm, d), lambda i, f: (i, 0)),
            _resident((1, d)),
            pl.BlockSpec((None, None, d, tf2), lambda i, f: (layer, f, 0, 0)),
            pl.BlockSpec((None, tf, d), lambda i, f: (layer, f, 0)),
            _resident((1, d)),
        ],
        out_specs=pl.BlockSpec((tm, d), lambda i, f: (i, 0)),
        out_shape=jax.ShapeDtypeStruct((m, d), F32),
        scratch_shapes=[pltpu.VMEM((tm, d), BF16)],
        compiler_params=_params("parallel", "arbitrary"),
        name="swiglu_sublayer",
    )(x, g_pre.reshape(1, d), w_gate_up, w_down, g_post.reshape(1, d))


def kernel(x, mem, norms, mem_norm, even_w_in, pool_w, pool_scale, even_w_out,
           odd_w_in, gla_gate_w, gla_gate_b, gla_gnorm, odd_w_out,
           xattn_wq, xattn_wk, xattn_wv, xattn_wo,
           ffn_w_gate, ffn_w_up, ffn_w_down):
    batch, seq, d = x.shape
    n_mem = mem.shape[1]
    depth = norms.shape[0]
    t = batch * seq
    tp = TILE_PLAN
    bf = lambda w: w.astype(BF16)

    n_main = 2 * GLA_HEADS * GLA_HK + 2 * GLA_HEADS * GLA_HV
    w_in_even = cast_bf16(even_w_in, tr=tp.cast_rows)
    later_weights = (ffn_w_gate, ffn_w_up, ffn_w_down, odd_w_in, odd_w_out, n_main, tp.ffn_hidden)

    xs = x.reshape(t, d)
    mem2 = mem.reshape(batch * n_mem, d)
    mem_k = memory_projection(mem2, mem_norm, xattn_wk)
    mem_v = memory_projection(mem2, mem_norm, xattn_wv)

    for layer in range(depth):
        g = norms[layer]
        i = layer // 2
        if layer % 2 == 0:
            if layer == 0:
                proj, wq, wo, w_out_even = norm_matmul(xs, g[0], w_in_even[i], tm=tp.even_in_rows,
                                                       side_casts=(xattn_wq, xattn_wo, even_w_out))
            else:
                proj = norm_matmul(xs, g[0], w_in_even[i], tm=tp.even_in_rows)
            o_sb = stick_breaking_attention(proj, batch=batch, seq=seq, blk=tp.sb_block,
                                            nsub=tp.sb_blocks_per_step)
            xs = even_mixer_output(o_sb, proj, bf(pool_w[i]), pool_scale[i], w_out_even[i], g[1], xs,
                                   seq=seq, tm=tp.even_out_rows, sub=tp.even_out_dot_rows)
        else:
            gate_w = jnp.pad(bf(gla_gate_w[i]), ((0, LANES - GLA_GATE_RANK), (0, 0)))
            proj, log_gate = norm_matmul(xs, g[0], w_in_odd[i], (w_a_odd[i], gate_w, gla_gate_b[i]),
                                         tm=tp.odd_in_rows, swish_from=n_main - GLA_HEADS * GLA_HV)
            o = gated_linear_attention(proj, log_gate, batch=batch, seq=seq, ts=tp.gla_rows)
            xs = gated_matmul_norm_res(o, proj, n_main // o.shape[1] - 1, gla_gnorm[i], w_out_odd[i],
                                       g[1], xs, heads=GLA_HEADS, tm=tp.odd_out_rows,
                                       sub=tp.odd_out_dot_rows)
        out = cross_attention_sublayer(xs, g[2], wq, mem_k, mem_v, layer, wo, g[3],
                                       batch=batch, seq=seq, n_mem=n_mem, tm=tp.xattn_rows,
                                       sub=tp.xattn_rows, prep=later_weights if layer == 0 else None)
        if layer == 0:
            xs, w_gate_up, w_down, w_in_odd, w_a_odd, w_out_odd = out
        else:
            xs = out
        xs = swiglu_sublayer(xs, g[4], w_gate_up, w_down, layer, g[5], tm=tp.ffn_rows,
                             sub=tp.ffn_dot_rows)
    return xs.reshape(batch, seq, d)
```
